```python
import math
import jax, jax.numpy as jnp
from jax import lax
import numpy as np

D_MODEL = 2048
BATCH = 2
SEQ = 4096
DEPTH = 2
DEC_BATCH = 128
DEC_SEQ = 8
PAST_LEN = 2048
PAGE_SIZE = 128

D_HEAD = 128
H_A = 8
H_B = 4
H_C = 4
DK_C = 64
DV_C = 128
W_A = H_A * D_HEAD
W_B = H_B * D_HEAD
WK_C = H_C * DK_C
WV_C = H_C * DV_C
GLA_RANK = 16
GLA_GATE_NORM = 16.0
GLA_CHUNK = 16
MOBA_BLOCK = 256
MOBA_TOPK = 3
MOBA_QC = 64
SB_QB = 128
RMS_EPS = 1e-6
SPLIT_SIZES = (W_A, W_A, W_A, W_A, W_B, W_B, W_B, W_B, WK_C, WK_C, WV_C, WV_C, GLA_RANK, D_MODEL, D_MODEL, D_MODEL)
PROJ_WIDTH = 4 * W_A + 4 * W_B + 2 * WK_C + 2 * WV_C + GLA_RANK + 3 * D_MODEL

kernel_name = 'moba_stickbreak_gla_gated_hybrid_step'


def rms_norm(x, g):
    x32 = x.astype(jnp.float32)
    y = x32 * lax.rsqrt(jnp.mean(x32 * x32, axis=-1, keepdims=True) + RMS_EPS)
    return (y * g.astype(jnp.float32)).astype(x.dtype)


def project(h, w_in, gla_w2, gla_b):
    N, L, _ = h.shape
    z = h @ w_in
    offs = np.cumsum(SPLIT_SIZES)[:-1].tolist()
    (qa, ka, va, ga, qb, kb, vb, gb, qc, kc, vc, gc, rc, ma, mb, mc) = jnp.split(z, offs, axis=-1)
    heads = lambda t, n: t.reshape(N, L, n, t.shape[-1] // n)
    log_a = jax.nn.log_sigmoid((rc @ gla_w2 + gla_b).astype(jnp.float32)) / GLA_GATE_NORM
    br_a = (heads(qa, H_A), heads(ka, H_A), heads(va, H_A), ga)
    br_b = (heads(qb, H_B), heads(kb, H_B), heads(vb, H_B), gb)
    br_c = (heads(qc, H_C), heads(kc, H_C), heads(vc, H_C), heads(log_a, H_C), gc)
    return br_a, br_b, br_c, (ma, mb, mc)


def merge(oa, ga, ob, gb, oc, gc, gates, gla_out_gain, w_br_a, w_br_b, w_br_c, w_out):
    N, L = ga.shape[:2]
    ya = (oa.reshape(N, L, W_A) * jax.nn.silu(ga)) @ w_br_a
    yb = (ob.reshape(N, L, W_B) * jax.nn.silu(gb)) @ w_br_b
    oc = rms_norm(oc, gla_out_gain)
    yc = (oc.reshape(N, L, WV_C) * jax.nn.silu(gc)) @ w_br_c
    ma, mb, mc = gates
    m = jax.nn.sigmoid(ma) * ya + jax.nn.sigmoid(mb) * yb + jax.nn.sigmoid(mc) * yc
    return m @ w_out


def gather_pages(pool, pages):
    g = pool[pages]
    return g.reshape(pages.shape[0], pages.shape[1] * pool.shape[1], pool.shape[2], pool.shape[3])


def moba_prompt(q, k, v):
    N, L, H, Dh = q.shape
    scale = Dh ** -0.5
    nbc = L // MOBA_BLOCK
    nqc = L // MOBA_QC
    kk = min(MOBA_TOPK, nbc)
    pad = (-L) % MOBA_BLOCK
    k_pad = jnp.pad(k, ((0, 0), (0, pad), (0, 0), (0, 0)))
    v_pad = jnp.pad(v, ((0, 0), (0, pad), (0, 0), (0, 0)))
    to_chunks = lambda t: t.reshape((N, nqc, MOBA_QC) + t.shape[2:]).swapaxes(0, 1)
    xs = (jnp.arange(nqc), to_chunks(q))
    kb = vb = None
    bi = jnp.arange(N)[:, None, None, None]
    hi = jnp.arange(H)[None, None, :, None]
    if kk > 0:
        kb = k[:, :nbc * MOBA_BLOCK].reshape(N, nbc, MOBA_BLOCK, H, Dh)
        vb = v[:, :nbc * MOBA_BLOCK].reshape(N, nbc, MOBA_BLOCK, H, Dh)
        means = jnp.mean(kb.astype(jnp.float32), axis=2)
        own = jnp.arange(L) // MOBA_BLOCK
        sc = jnp.einsum('nlhd,nbhd->nlhb', q.astype(jnp.float32), means)
        allowed = jnp.arange(nbc)[None, :] < own[:, None]
        sc = jnp.where(allowed[None, :, None, :], sc, -jnp.inf)
        _, idx = lax.top_k(sc, kk)
        valid = idx < own[None, :, None, None]
        xs = xs + (to_chunks(idx), to_chunks(valid))

    def chunk(args):
        c, q_c = args[0], args[1]
        q0 = c * MOBA_QC
        blk0 = (q0 // MOBA_BLOCK) * MOBA_BLOCK
        k_own = lax.dynamic_slice_in_dim(k_pad, blk0, MOBA_BLOCK, axis=1)
        v_own = lax.dynamic_slice_in_dim(v_pad, blk0, MOBA_BLOCK, axis=1)
        qpos = q0 + jnp.arange(MOBA_QC)
        kpos = blk0 + jnp.arange(MOBA_BLOCK)
        lo = jnp.einsum('nqhd,nkhd->nhqk', q_c, k_own, preferred_element_type=jnp.float32) * scale
        lo = jnp.where(kpos[None, :] <= qpos[:, None], lo, -jnp.inf)
        if kk == 0:
            p = jax.nn.softmax(lo, axis=-1).astype(v.dtype)
            return jnp.einsum('nhqk,nkhd->nqhd', p, v_own)
        idx_c, valid_c = args[2], args[3]
        kg = kb[bi, idx_c, :, hi]
        vg = vb[bi, idx_c, :, hi]
        lp = jnp.einsum('nqhd,nqhjkd->nhqjk', q_c, kg, preferred_element_type=jnp.float32) * scale
        lp = jnp.where(valid_c.transpose(0, 2, 1, 3)[..., None], lp, -jnp.inf)
        lp = lp.reshape(N, H, MOBA_QC, kk * MOBA_BLOCK)
        p = jax.nn.softmax(jnp.concatenate([lp, lo], axis=-1), axis=-1).astype(v.dtype)
        pp = p[..., :kk * MOBA_BLOCK].reshape(N, H, MOBA_QC, kk, MOBA_BLOCK)
        po = p[..., kk * MOBA_BLOCK:]
        return (jnp.einsum('nhqjk,nqhjkd->nqhd', pp, vg)
                + jnp.einsum('nhqk,nkhd->nqhd', po, v_own))

    out = lax.map(chunk, xs)
    return out.swapaxes(0, 1).reshape(N, L, H, Dh)


def moba_sample(q, k_new, v_new, pool_k, pool_v, page_table):
    DB, T, H, Dh = q.shape
    scale = Dh ** -0.5
    n_pages = page_table.shape[1]
    past = n_pages * PAGE_SIZE
    nfull = past // MOBA_BLOCK
    ppb = MOBA_BLOCK // PAGE_SIZE
    kk = min(MOBA_TOPK, nfull)
    own_pages = page_table[:, nfull * ppb:]
    k_own = jnp.concatenate([gather_pages(pool_k, own_pages), k_new], axis=1)
    v_own = jnp.concatenate([gather_pages(pool_v, own_pages), v_new], axis=1)
    qpos = past + jnp.arange(T)
    kpos = nfull * MOBA_BLOCK + jnp.arange(k_own.shape[1])
    lo = jnp.einsum('nqhd,nkhd->nhqk', q, k_own, preferred_element_type=jnp.float32) * scale
    lo = jnp.where(kpos[None, :] <= qpos[:, None], lo, -jnp.inf)
    if kk == 0:
        p = jax.nn.softmax(lo, axis=-1).astype(v_new.dtype)
        return jnp.einsum('nhqk,nkhd->nqhd', p, v_own)
    k_full = gather_pages(pool_k, page_table[:, :nfull * ppb]).reshape(DB, nfull, MOBA_BLOCK, H, Dh)
    means = jnp.mean(k_full.astype(jnp.float32), axis=2)
    sc = jnp.einsum('nqhd,nbhd->nqhb', q.astype(jnp.float32), means)
    _, idx = lax.top_k(sc, kk)
    bi = jnp.arange(DB)[:, None, None, None, None]
    phys = page_table[bi, idx[..., None] * ppb + jnp.arange(ppb)]
    hi = jnp.arange(H)[None, None, :, None, None]
    kg = pool_k[phys, :, hi].reshape(DB, T, H, kk, MOBA_BLOCK, Dh)
    vg = pool_v[phys, :, hi].reshape(DB, T, H, kk, MOBA_BLOCK, Dh)
    lp = jnp.einsum('nqhd,nqhjkd->nhqjk', q, kg, preferred_element_type=jnp.float32) * scale
    lp = lp.reshape(DB, H, T, kk * MOBA_BLOCK)
    p = jax.nn.softmax(jnp.concatenate([lp, lo], axis=-1), axis=-1).astype(v_new.dtype)
    pp = p[..., :kk * MOBA_BLOCK].reshape(DB, H, T, kk, MOBA_BLOCK)
    po = p[..., kk * MOBA_BLOCK:]
    return (jnp.einsum('nhqjk,nqhjkd->nqhd', pp, vg)
            + jnp.einsum('nhqk,nkhd->nqhd', po, v_own))


def stick_breaking(z, qpos, kpos, v):
    causal = kpos[None, :] < qpos[:, None]
    log_keep = jnp.where(causal, jax.nn.log_sigmoid(-z), 0.0)
    after = lax.cumsum(log_keep, axis=3, reverse=True) - log_keep
    a = jnp.where(causal, jnp.exp(jax.nn.log_sigmoid(z) + after), 0.0)
    return jnp.einsum('nhqk,nkhd->nqhd', a.astype(v.dtype), v)


def sb_prompt(q, k, v):
    N, L, H, Dh = q.shape
    scale = Dh ** -0.5
    nqb = L // SB_QB
    kpos = jnp.arange(L)
    qs = q.reshape(N, nqb, SB_QB, H, Dh).swapaxes(0, 1)

    def block(args):
        c, q_c = args
        qpos = c * SB_QB + jnp.arange(SB_QB)
        z = jnp.einsum('nqhd,nkhd->nhqk', q_c, k, preferred_element_type=jnp.float32) * scale
        return stick_breaking(z, qpos, kpos, v)

    out = lax.map(block, (jnp.arange(nqb), qs))
    return out.swapaxes(0, 1).reshape(N, L, H, Dh)


def sb_sample(q, k_new, v_new, pool_k, pool_v, page_table):
    DB, T, H, Dh = q.shape
    scale = Dh ** -0.5
    past = page_table.shape[1] * PAGE_SIZE
    k_all = jnp.concatenate([gather_pages(pool_k, page_table), k_new], axis=1)
    v_all = jnp.concatenate([gather_pages(pool_v, page_table), v_new], axis=1)
    qpos = past + jnp.arange(T)
    kpos = jnp.arange(past + T)
    z = jnp.einsum('nqhd,nkhd->nhqk', q, k_all, preferred_element_type=jnp.float32) * scale
    return stick_breaking(z, qpos, kpos, v_all)


def gla_chunked(q, k, v, log_a, s0):
    out_dtype = v.dtype
    N, L, H, dk = q.shape
    dv = v.shape[-1]
    c = math.gcd(L, GLA_CHUNK)
    nc = L // c
    rs = lambda t: t.astype(jnp.float32).reshape(N, nc, c, H, t.shape[-1])
    q = rs(q) * dk ** -0.5
    k, v, log_a = rs(k), rs(v), rs(log_a)
    b = jnp.cumsum(log_a, axis=2)
    tri = jnp.arange(c)[:, None] >= jnp.arange(c)[None, :]
    diff = b[:, :, :, None] - b[:, :, None, :]
    decay = jnp.exp(jnp.where(tri[None, None, :, :, None, None], diff, -jnp.inf))
    attn = jnp.einsum('nctshk,ncshk->nchts', q[:, :, :, None] * decay, k)
    o = jnp.einsum('nchts,ncshv->ncthv', attn, v)
    b_last = b[:, :, -1]
    kd = k * jnp.exp(b_last[:, :, None] - b)
    ds = jnp.einsum('ncshk,ncshv->nchkv', kd, v)
    g = jnp.exp(b_last)

    def step(s, inp):
        g_c, ds_c = inp
        return g_c[..., None] * s + ds_c, s

    s_fin, s_in = lax.scan(step, s0, (g.swapaxes(0, 1), ds.swapaxes(0, 1)))
    s_in = s_in.swapaxes(0, 1)
    o = o + jnp.einsum('ncthk,nchkv->ncthv', q * jnp.exp(b), s_in)
    return o.reshape(N, L, H, dv).astype(out_dtype), s_fin


def setup_inputs(seed: int = 0) -> dict:
    key = jax.random.key(seed)
    ks = jax.random.split(key, 20)
    n_pages = PAST_LEN // PAGE_SIZE
    n_used = DEC_BATCH * n_pages
    n_pool = n_used + max(1, n_used // 4)
    nrm = jax.random.normal
    f32 = jnp.float32
    page_table = jax.random.permutation(ks[7], n_pool)[:n_used].reshape(DEC_BATCH, n_pages).astype(jnp.int32)
    return {
        'x_prompt': nrm(ks[0], (BATCH, SEQ, D_MODEL), f32),
        'x_sample': nrm(ks[1], (DEC_BATCH, DEC_SEQ, D_MODEL), f32),
        'cache_moba_k': nrm(ks[2], (DEPTH, n_pool, PAGE_SIZE, H_A, D_HEAD), f32),
        'cache_moba_v': nrm(ks[3], (DEPTH, n_pool, PAGE_SIZE, H_A, D_HEAD), f32),
        'cache_sb_k': nrm(ks[4], (DEPTH, n_pool, PAGE_SIZE, H_B, D_HEAD), f32),
        'cache_sb_v': nrm(ks[5], (DEPTH, n_pool, PAGE_SIZE, H_B, D_HEAD), f32),
        'state_gla': 0.5 * nrm(ks[6], (DEPTH, DEC_BATCH, H_C, DK_C, DV_C), f32),
        'page_table': page_table,
        'norm_gain': 1.0 + 0.05 * nrm(ks[8], (DEPTH, D_MODEL), f32),
        'w_in': nrm(ks[9], (DEPTH, D_MODEL, PROJ_WIDTH), f32) * D_MODEL ** -0.5,
        'gla_w2': nrm(ks[10], (DEPTH, GLA_RANK, WK_C), f32) * GLA_RANK ** -0.5,
        'gla_b': 0.1 * nrm(ks[11], (DEPTH, WK_C), f32),
        'gla_out_gain': 1.0 + 0.05 * nrm(ks[12], (DEPTH, DV_C), f32),
        'w_br_a': nrm(ks[13], (DEPTH, W_A, D_MODEL), f32) * W_A ** -0.5,
        'w_br_b': nrm(ks[14], (DEPTH, W_B, D_MODEL), f32) * W_B ** -0.5,
        'w_br_c': nrm(ks[15], (DEPTH, WV_C, D_MODEL), f32) * WV_C ** -0.5,
        'w_out': nrm(ks[16], (DEPTH, D_MODEL, D_MODEL), f32) * D_MODEL ** -0.5,
        'final_gain': 1.0 + 0.05 * nrm(ks[17], (D_MODEL,), f32),
    }


def reference(x_prompt, x_sample, cache_moba_k, cache_moba_v, cache_sb_k, cache_sb_v, state_gla, page_table,
              norm_gain, w_in, gla_w2, gla_b, gla_out_gain, w_br_a, w_br_b, w_br_c, w_out, final_gain):
    xp, xs = x_prompt, x_sample
    pka, pva, pkb, pvb, psc = [], [], [], [], []
    ska, sva, skb, svb, ssc = [], [], [], [], []
    for l in range(DEPTH):
        lw = (gla_out_gain[l], w_br_a[l], w_br_b[l], w_br_c[l], w_out[l])
        br_a, br_b, br_c, gates = project(rms_norm(xp, norm_gain[l]), w_in[l], gla_w2[l], gla_b[l])
        oa = moba_prompt(br_a[0], br_a[1], br_a[2])
        ob = sb_prompt(br_b[0], br_b[1], br_b[2])
        s0 = jnp.zeros((xp.shape[0], H_C, DK_C, DV_C), jnp.float32)
        oc, s_fin = gla_chunked(br_c[0], br_c[1], br_c[2], br_c[3], s0)
        xp = xp + merge(oa, br_a[3], ob, br_b[3], oc, br_c[4], gates, *lw)
        pka.append(br_a[1]); pva.append(br_a[2]); pkb.append(br_b[1]); pvb.append(br_b[2]); psc.append(s_fin)
        br_a, br_b, br_c, gates = project(rms_norm(xs, norm_gain[l]), w_in[l], gla_w2[l], gla_b[l])
        oa = moba_sample(br_a[0], br_a[1], br_a[2], cache_moba_k[l], cache_moba_v[l], page_table)
        ob = sb_sample(br_b[0], br_b[1], br_b[2], cache_sb_k[l], cache_sb_v[l], page_table)
        oc, s_fin = gla_chunked(br_c[0], br_c[1], br_c[2], br_c[3], state_gla[l].astype(jnp.float32))
        xs = xs + merge(oa, br_a[3], ob, br_b[3], oc, br_c[4], gates, *lw)
        ska.append(br_a[1]); sva.append(br_a[2]); skb.append(br_b[1]); svb.append(br_b[2]); ssc.append(s_fin)
    return (rms_norm(xp, final_gain), rms_norm(xs, final_gain),
            jnp.stack(pka), jnp.stack(pva), jnp.stack(pkb), jnp.stack(pvb), jnp.stack(psc),
            jnp.stack(ska), jnp.stack(sva), jnp.stack(skb), jnp.stack(svb), jnp.stack(ssc))
```

```python
import functools
import math

import numpy as np
import jax
import jax.numpy as jnp
from jax import lax
from jax.experimental import pallas as pl
from jax.experimental.pallas import tpu as pltpu

D_MODEL = 2048
DEPTH = 2
PAGE_SIZE = 128
D_HEAD = 128
H_A = 8
H_B = 4
H_C = 4
DK_C = 64
DV_C = 128
W_A = H_A * D_HEAD
W_B = H_B * D_HEAD
WK_C = H_C * DK_C
WV_C = H_C * DV_C
GLA_RANK = 16
GLA_GATE_NORM = 16.0
GLA_CHUNK = 16
MOBA_BLOCK = 256
MOBA_TOPK = 3
RMS_EPS = 1e-6

LANE = 128
MAIN_W = 4 * W_A + 4 * W_B + 2 * WK_C + 2 * WV_C
GATE_W = 3 * D_MODEL
RC_PAD = 512
Z_W = MAIN_W + GATE_W + RC_PAD
CB_QA, CB_KA, CB_VA, CB_GA = 0, 8, 16, 24
CB_QB, CB_KB, CB_VB, CB_GB = 32, 36, 40, 44
CB_QC, CB_KC, CB_VC, CB_GC = 48, 50, 52, 56
CB_MA, CB_MB, CB_MC = 60, 76, 92
CB_RC = 108

NEG = -1e30
VMEM_LIMIT = 48 * 1024 * 1024

F32 = jnp.float32
BF16 = jnp.bfloat16


def _dot_t(a, b, **kw):
    return lax.dot_general(a, b, (((1,), (1,)), ((), ())), preferred_element_type=F32, **kw)


def _proj_kernel(x_ref, g_ref, w_ref, o_ref, xn_ref):
    @pl.when(pl.program_id(1) == 0)
    def _():
        x = x_ref[...]
        ms = jnp.mean(x * x, axis=-1, keepdims=True)
        xn_ref[...] = (x * lax.rsqrt(ms + RMS_EPS) * g_ref[...]).astype(BF16)

    o_ref[...] = jnp.dot(xn_ref[...], w_ref[...], preferred_element_type=F32)


def norm_project(x, gain, w_bf16, *, tm=1024, tn=1024):
    m, d = x.shape
    n = w_bf16.shape[1]
    return pl.pallas_call(
        _proj_kernel,
        grid=(m // tm, n // tn),
        in_specs=[
            pl.BlockSpec((tm, d), lambda i, j: (i, 0)),
            pl.BlockSpec((1, d), lambda i, j: (0, 0)),
            pl.BlockSpec((d, tn), lambda i, j: (0, j)),
        ],
        out_specs=pl.BlockSpec((tm, tn), lambda i, j: (i, j)),
        out_shape=jax.ShapeDtypeStruct((m, n), F32),
        scratch_shapes=[pltpu.VMEM((tm, d), BF16)],
        compiler_params=pltpu.CompilerParams(
            dimension_semantics=("parallel", "arbitrary"), vmem_limit_bytes=VMEM_LIMIT),
        name="norm_project",
    )(x, gain.reshape(1, d), w_bf16)


def _moba_prompt_kernel(q_ref, k_ref, v_ref, o_ref, means_ref, bias_ref, *, nblk):
    qi = pl.program_id(2)
    blk = MOBA_BLOCK
    scale = D_HEAD ** -0.5

    @pl.when(qi == 0)
    def _():
        for j in range(nblk):
            means_ref[j:j + 1, :] = jnp.sum(k_ref[j * blk:(j + 1) * blk, :], axis=0, keepdims=True) * (1.0 / blk)

    q = q_ref[...]
    sc = _dot_t(q, means_ref[...], precision=lax.Precision.HIGHEST)
    col = lax.broadcasted_iota(jnp.int32, sc.shape, 1)
    allowed = col < qi
    sc = jnp.where(allowed, sc, -jnp.inf)
    rank = jnp.zeros(sc.shape, jnp.int32)
    for i in range(nblk):
        si = sc[:, i:i + 1]
        beats = (si > sc) | ((si == sc) & (col > i))
        rank = rank + beats.astype(jnp.int32)
    sel = (rank < MOBA_TOPK) & allowed
    for j in range(nblk):
        bias_ref[j] = jnp.broadcast_to(jnp.where(sel[:, j:j + 1], 0.0, NEG), (blk, LANE))

    qs = (q * scale).astype(BF16)
    d0 = pl.multiple_of(qi * blk, blk)
    kd = k_ref[pl.ds(d0, blk), :].astype(BF16)
    vd = v_ref[pl.ds(d0, blk), :].astype(BF16)
    s = _dot_t(qs, kd)
    r_i = lax.broadcasted_iota(jnp.int32, s.shape, 0)
    c_i = lax.broadcasted_iota(jnp.int32, s.shape, 1)
    s = jnp.where(c_i <= r_i, s, NEG)
    m = jnp.max(s, axis=-1, keepdims=True)
    p = jnp.exp(s - m)
    l = jnp.sum(p, axis=-1, keepdims=True)
    acc = jnp.dot(p.astype(BF16), vd, preferred_element_type=F32)

    def body(j, carry):
        m, l, acc = carry
        j0 = pl.multiple_of(j * blk, blk)
        kj = k_ref[pl.ds(j0, blk), :].astype(BF16)
        vj = v_ref[pl.ds(j0, blk), :].astype(BF16)
        b = bias_ref[j]
        s = _dot_t(qs, kj) + jnp.concatenate([b] * (blk // LANE), axis=-1)
        m_new = jnp.maximum(m, jnp.max(s, axis=-1, keepdims=True))
        alpha = jnp.exp(m - m_new)
        p = jnp.exp(s - m_new)
        l = alpha * l + jnp.sum(p, axis=-1, keepdims=True)
        acc = alpha * acc + jnp.dot(p.astype(BF16), vj, preferred_element_type=F32)
        return m_new, l, acc

    m, l, acc = lax.fori_loop(0, qi, body, (m, l, acc))
    o_ref[...] = acc / l


def moba_prompt_attn(z, n_seq, seq_len):
    blk = MOBA_BLOCK
    nblk = seq_len // blk
    return pl.pallas_call(
        functools.partial(_moba_prompt_kernel, nblk=nblk),
        grid=(n_seq, H_A, nblk),
        in_specs=[
            pl.BlockSpec((blk, D_HEAD), lambda n, h, i: (n * nblk + i, CB_QA + h)),
            pl.BlockSpec((seq_len, D_HEAD), lambda n, h, i: (n, CB_KA + h)),
            pl.BlockSpec((seq_len, D_HEAD), lambda n, h, i: (n, CB_VA + h)),
        ],
        out_specs=pl.BlockSpec((blk, D_HEAD), lambda n, h, i: (n * nblk + i, h)),
        out_shape=jax.ShapeDtypeStruct((n_seq * seq_len, W_A), F32),
        scratch_shapes=[pltpu.VMEM((nblk, D_HEAD), F32), pltpu.VMEM((nblk, blk, LANE), F32)],
        compiler_params=pltpu.CompilerParams(
            dimension_semantics=("parallel", "parallel", "arbitrary"), vmem_limit_bytes=VMEM_LIMIT),
        name="moba_prompt",
    )(z, z, z)


def _sb_prompt_kernel(q_ref, k_ref, v_ref, o_ref, *, blk):
    qi = pl.program_id(2)
    scale = D_HEAD ** -0.5
    qs = (q_ref[...] * scale).astype(BF16)
    r_i = lax.broadcasted_iota(jnp.int32, (blk, blk), 0)
    c_i = lax.broadcasted_iota(jnp.int32, (blk, blk), 1)
    upper = jnp.where(r_i > c_i, 1.0, 0.0).astype(BF16)
    causal = c_i < r_i

    def block(j, c, acc, diag):
        j0 = pl.multiple_of(j * blk, blk)
        kj = k_ref[pl.ds(j0, blk), :].astype(BF16)
        vj = v_ref[pl.ds(j0, blk), :].astype(BF16)
        z = _dot_t(qs, kj)
        sp = jnp.maximum(z, 0.0) + jnp.log(1.0 + jnp.exp(-jnp.abs(z)))
        lk = -sp
        if diag:
            lk = jnp.where(causal, lk, 0.0)
        hi = lk.astype(BF16)
        lo = (lk - hi.astype(F32)).astype(BF16)
        after = (jnp.dot(hi, upper, preferred_element_type=F32)
                 + jnp.dot(lo, upper, preferred_element_type=F32))
        a = jnp.exp((z - sp) + after + c)
        if diag:
            a = jnp.where(causal, a, 0.0)
        acc = acc + jnp.dot(a.astype(BF16), vj, preferred_element_type=F32)
        c = c + after[:, 0:1] + lk[:, 0:1]
        return c, acc

    c0 = jnp.zeros((blk, 1), F32)
    acc0 = jnp.zeros((blk, D_HEAD), F32)
    c, acc = block(qi, c0, acc0, True)

    def body(t, carry):
        return block(qi - 1 - t, carry[0], carry[1], False)

    c, acc = lax.fori_loop(0, qi, body, (c, acc))
    o_ref[...] = acc


def sb_prompt_attn(z, n_seq, seq_len, *, blk=256):
    nblk = seq_len // blk
    return pl.pallas_call(
        functools.partial(_sb_prompt_kernel, blk=blk),
        grid=(n_seq, H_B, nblk),
        in_specs=[
            pl.BlockSpec((blk, D_HEAD), lambda n, h, i: (n * nblk + i, CB_QB + h)),
            pl.BlockSpec((seq_len, D_HEAD), lambda n, h, i: (n, CB_KB + h)),
            pl.BlockSpec((seq_len, D_HEAD), lambda n, h, i: (n, CB_VB + h)),
        ],
        out_specs=pl.BlockSpec((blk, D_HEAD), lambda n, h, i: (n * nblk + i, h)),
        out_shape=jax.ShapeDtypeStruct((n_seq * seq_len, W_B), F32),
        compiler_params=pltpu.CompilerParams(
            dimension_semantics=("parallel", "parallel", "arbitrary"), vmem_limit_bytes=VMEM_LIMIT),
        name="sb_prompt",
    )(z, z, z)


def rms_norm(x, g):
    x32 = x.astype(F32)
    y = x32 * lax.rsqrt(jnp.mean(x32 * x32, axis=-1, keepdims=True) + RMS_EPS)
    return (y * g.astype(F32)).astype(x.dtype)


def gather_pages(pool, pages):
    g = pool[pages]
    return g.reshape(pages.shape[0], pages.shape[1] * pool.shape[1], pool.shape[2], pool.shape[3])


def moba_sample(q, k_new, v_new, pool_k, pool_v, page_table):
    DB, T, H, Dh = q.shape
    scale = Dh ** -0.5
    n_pages = page_table.shape[1]
    past = n_pages * PAGE_SIZE
    nfull = past // MOBA_BLOCK
    ppb = MOBA_BLOCK // PAGE_SIZE
    kk = min(MOBA_TOPK, nfull)
    own_pages = page_table[:, nfull * ppb:]
    k_own = jnp.concatenate([gather_pages(pool_k, own_pages), k_new], axis=1)
    v_own = jnp.concatenate([gather_pages(pool_v, own_pages), v_new], axis=1)
    qpos = past + jnp.arange(T)
    kpos = nfull * MOBA_BLOCK + jnp.arange(k_own.shape[1])
    lo = jnp.einsum('nqhd,nkhd->nhqk', q, k_own, preferred_element_type=F32) * scale
    lo = jnp.where(kpos[None, :] <= qpos[:, None], lo, -jnp.inf)
    k_full = gather_pages(pool_k, page_table[:, :nfull * ppb]).reshape(DB, nfull, MOBA_BLOCK, H, Dh)
    means = jnp.mean(k_full.astype(F32), axis=2)
    sc = jnp.einsum('nqhd,nbhd->nqhb', q.astype(F32), means)
    _, idx = lax.top_k(sc, kk)
    bi = jnp.arange(DB)[:, None, None, None, None]
    phys = page_table[bi, idx[..., None] * ppb + jnp.arange(ppb)]
    hi = jnp.arange(H)[None, None, :, None, None]
    kg = pool_k[phys, :, hi].reshape(DB, T, H, kk, MOBA_BLOCK, Dh)
    vg = pool_v[phys, :, hi].reshape(DB, T, H, kk, MOBA_BLOCK, Dh)
    lp = jnp.einsum('nqhd,nqhjkd->nhqjk', q, kg, preferred_element_type=F32) * scale
    lp = lp.reshape(DB, H, T, kk * MOBA_BLOCK)
    p = jax.nn.softmax(jnp.concatenate([lp, lo], axis=-1), axis=-1).astype(v_new.dtype)
    pp = p[..., :kk * MOBA_BLOCK].reshape(DB, H, T, kk, MOBA_BLOCK)
    po = p[..., kk * MOBA_BLOCK:]
    return (jnp.einsum('nhqjk,nqhjkd->nqhd', pp, vg)
            + jnp.einsum('nhqk,nkhd->nqhd', po, v_own))


def stick_breaking(z, qpos, kpos, v):
    causal = kpos[None, :] < qpos[:, None]
    log_keep = jnp.where(causal, jax.nn.log_sigmoid(-z), 0.0)
    after = lax.cumsum(log_keep, axis=3, reverse=True) - log_keep
    a = jnp.where(causal, jnp.exp(jax.nn.log_sigmoid(z) + after), 0.0)
    return jnp.einsum('nhqk,nkhd->nqhd', a.astype(v.dtype), v)


def sb_sample(q, k_new, v_new, pool_k, pool_v, page_table):
    DB, T, H, Dh = q.shape
    scale = Dh ** -0.5
    past = page_table.shape[1] * PAGE_SIZE
    k_all = jnp.concatenate([gather_pages(pool_k, page_table), k_new], axis=1)
    v_all = jnp.concatenate([gather_pages(pool_v, page_table), v_new], axis=1)
    qpos = past + jnp.arange(T)
    kpos = jnp.arange(past + T)
    z = jnp.einsum('nqhd,nkhd->nhqk', q, k_all, preferred_element_type=F32) * scale
    return stick_breaking(z, qpos, kpos, v_all)


def gla_chunked(q, k, v, log_a, s0):
    out_dtype = v.dtype
    N, L, H, dk = q.shape
    dv = v.shape[-1]
    c = math.gcd(L, GLA_CHUNK)
    nc = L // c
    rs = lambda t: t.astype(F32).reshape(N, nc, c, H, t.shape[-1])
    q = rs(q) * dk ** -0.5
    k, v, log_a = rs(k), rs(v), rs(log_a)
    b = jnp.cumsum(log_a, axis=2)
    tri = jnp.arange(c)[:, None] >= jnp.arange(c)[None, :]
    diff = b[:, :, :, None] - b[:, :, None, :]
    decay = jnp.exp(jnp.where(tri[None, None, :, :, None, None], diff, -jnp.inf))
    attn = jnp.einsum('nctshk,ncshk->nchts', q[:, :, :, None] * decay, k)
    o = jnp.einsum('nchts,ncshv->ncthv', attn, v)
    b_last = b[:, :, -1]
    kd = k * jnp.exp(b_last[:, :, None] - b)
    ds = jnp.einsum('ncshk,ncshv->nchkv', kd, v)
    g = jnp.exp(b_last)

    def step(s, inp):
        g_c, ds_c = inp
        return g_c[..., None] * s + ds_c, s

    s_fin, s_in = lax.scan(step, s0, (g.swapaxes(0, 1), ds.swapaxes(0, 1)))
    s_in = s_in.swapaxes(0, 1)
    o = o + jnp.einsum('ncthk,nchkv->ncthv', q * jnp.exp(b), s_in)
    return o.reshape(N, L, H, dv).astype(out_dtype), s_fin


def merge(oa, ga, ob, gb, oc, gc, gates, gla_out_gain, w_br_a, w_br_b, w_br_c, w_out):
    N, L = ga.shape[:2]
    ya = (oa.reshape(N, L, W_A) * jax.nn.silu(ga)) @ w_br_a
    yb = (ob.reshape(N, L, W_B) * jax.nn.silu(gb)) @ w_br_b
    oc = rms_norm(oc, gla_out_gain)
    yc = (oc.reshape(N, L, WV_C) * jax.nn.silu(gc)) @ w_br_c
    ma, mb, mc = gates
    m = jax.nn.sigmoid(ma) * ya + jax.nn.sigmoid(mb) * yb + jax.nn.sigmoid(mc) * yc
    return m @ w_out


def _cols(z, cb, width):
    return z[:, cb * LANE: cb * LANE + width]


def kernel(x_prompt, x_sample, cache_moba_k, cache_moba_v, cache_sb_k, cache_sb_v, state_gla, page_table,
           norm_gain, w_in, gla_w2, gla_b, gla_out_gain, w_br_a, w_br_b, w_br_c, w_out, final_gain):
    n_seq, seq_len, _ = x_prompt.shape
    n_dec, dec_len, _ = x_sample.shape
    mp = n_seq * seq_len
    ms = n_dec * dec_len
    x_all = jnp.concatenate([x_prompt.reshape(mp, D_MODEL), x_sample.reshape(ms, D_MODEL)], axis=0)
    w_all = jnp.concatenate(
        [w_in[:, :, :MAIN_W], w_in[:, :, MAIN_W + GLA_RANK:], w_in[:, :, MAIN_W:MAIN_W + GLA_RANK],
         jnp.zeros((DEPTH, D_MODEL, RC_PAD - GLA_RANK), w_in.dtype)], axis=-1).astype(BF16)

    pka, pva, pkb, pvb, psc = [], [], [], [], []
    ska, sva, skb, svb, ssc = [], [], [], [], []
    for l in range(DEPTH):
        z = norm_project(x_all, norm_gain[l], w_all[l])
        oa_p = moba_prompt_attn(z, n_seq, seq_len)
        ob_p = sb_prompt_attn(z, n_seq, seq_len)

        def grp(rows, n, t):
            zz = rows
            hd = lambda cb, nh, w: _cols(zz, cb, nh * w).reshape(n, t, nh, w)
            flat = lambda cb, w: _cols(zz, cb, w).reshape(n, t, w)
            rc = _cols(zz, CB_RC, GLA_RANK).reshape(n, t, GLA_RANK)
            log_a = jax.nn.log_sigmoid((rc @ gla_w2[l] + gla_b[l]).astype(F32)) / GLA_GATE_NORM
            return dict(
                qa=hd(CB_QA, H_A, D_HEAD), ka=hd(CB_KA, H_A, D_HEAD), va=hd(CB_VA, H_A, D_HEAD), ga=flat(CB_GA, W_A),
                qb=hd(CB_QB, H_B, D_HEAD), kb=hd(CB_KB, H_B, D_HEAD), vb=hd(CB_VB, H_B, D_HEAD), gb=flat(CB_GB, W_B),
                qc=hd(CB_QC, H_C, DK_C), kc=hd(CB_KC, H_C, DK_C), vc=hd(CB_VC, H_C, DV_C), gc=flat(CB_GC, WV_C),
                la=log_a.reshape(n, t, H_C, DK_C),
                gates=(flat(CB_MA, D_MODEL), flat(CB_MB, D_MODEL), flat(CB_MC, D_MODEL)))

        lw = (gla_out_gain[l], w_br_a[l], w_br_b[l], w_br_c[l], w_out[l])
        p = grp(z[:mp], n_seq, seq_len)
        s0 = jnp.zeros((n_seq, H_C, DK_C, DV_C), F32)
        oc, s_fin = gla_chunked(p['qc'], p['kc'], p['vc'], p['la'], s0)
        d_p = merge(oa_p.reshape(n_seq, seq_len, H_A, D_HEAD), p['ga'], ob_p.reshape(n_seq, seq_len, H_B, D_HEAD),
                    p['gb'], oc, p['gc'], p['gates'], *lw)
        pka.append(p['ka']); pva.append(p['va']); pkb.append(p['kb']); pvb.append(p['vb']); psc.append(s_fin)

        s = grp(z[mp:], n_dec, dec_len)
        oa = moba_sample(s['qa'], s['ka'], s['va'], cache_moba_k[l], cache_moba_v[l], page_table)
        ob = sb_sample(s['qb'], s['kb'], s['vb'], cache_sb_k[l], cache_sb_v[l], page_table)
        oc, s_fin = gla_chunked(s['qc'], s['kc'], s['vc'], s['la'], state_gla[l].astype(F32))
        d_s = merge(oa, s['ga'], ob, s['gb'], oc, s['gc'], s['gates'], *lw)
        ska.append(s['ka']); sva.append(s['va']); skb.append(s['kb']); svb.append(s['vb']); ssc.append(s_fin)

        x_all = x_all + jnp.concatenate([d_p.reshape(mp, D_MODEL), d_s.reshape(ms, D_MODEL)], axis=0)

    y = rms_norm(x_all, final_gain)
    return (y[:mp].reshape(n_seq, seq_len, D_MODEL), y[mp:].reshape(n_dec, dec_len, D_MODEL),
            jnp.stack(pka), jnp.stack(pva), jnp.stack(pkb), jnp.stack(pvb), jnp.stack(psc),
            jnp.stack(ska), jnp.stack(sva), jnp.stack(skb), jnp.stack(svb), jnp.stack(ssc))
```

```python
import functools
import math

import jax
import jax.numpy as jnp
from jax import lax
from jax.experimental import pallas as pl
from jax.experimental.pallas import tpu as pltpu

D_MODEL = 2048
DEPTH = 2
PAGE_SIZE = 128
D_HEAD = 128
H_A = 8
H_B = 4
H_C = 4
DK_C = 64
DV_C = 128
W_A = H_A * D_HEAD
W_B = H_B * D_HEAD
WK_C = H_C * DK_C
WV_C = H_C * DV_C
GLA_RANK = 16
GLA_GATE_NORM = 16.0
GLA_CHUNK = 16
MOBA_BLOCK = 256
MOBA_TOPK = 3
RMS_EPS = 1e-6

LANE = 128
SUBLANE = 8
MAIN_W = 4 * W_A + 4 * W_B + 2 * WK_C + 2 * WV_C
GATE_W = 3 * D_MODEL
RC_PAD = 512
Z_W = GATE_W + MAIN_W + RC_PAD
CB_MA, CB_MB, CB_MC = 0, 16, 32
CB_QA, CB_KA, CB_VA, CB_GA = 48, 56, 64, 72
CB_QB, CB_KB, CB_VB, CB_GB = 80, 84, 88, 92
CB_QC, CB_KC, CB_VC, CB_GC = 96, 98, 100, 104
CB_RC = 108

NEG = -1e30
SB_LOG_FLOOR = -104.0
VMEM_LIMIT = 48 * 1024 * 1024

F32 = jnp.float32
BF16 = jnp.bfloat16
HIGHEST = lax.Precision.HIGHEST


def _dot_t(a, b, **kw):
    return lax.dot_general(a, b, (((1,), (1,)), ((), ())), preferred_element_type=F32, **kw)


def _dot(a, b, **kw):
    return jnp.dot(a, b, preferred_element_type=F32, **kw)


def _softplus(z):
    return jnp.maximum(z, 0.0) + jnp.log(1.0 + jnp.exp(-jnp.abs(z)))


def _sigmoid(z):
    return 1.0 / (1.0 + jnp.exp(-z))


def _split_bf16(x):
    hi = x.astype(BF16)
    lo = (x - hi.astype(F32)).astype(BF16)
    return hi, lo


def _params(*sem):
    return pltpu.CompilerParams(dimension_semantics=sem, vmem_limit_bytes=VMEM_LIMIT)


def _proj_kernel(x_ref, g_ref, w_ref, o_ref, xn_ref):
    @pl.when(pl.program_id(1) == 0)
    def _():
        x = x_ref[...]
        ms = jnp.mean(x * x, axis=-1, keepdims=True)
        xn_ref[...] = (x * lax.rsqrt(ms + RMS_EPS) * g_ref[...]).astype(BF16)

    o_ref[...] = _dot(xn_ref[...], w_ref[...])


def norm_project(x, gain, w_bf16, *, tm=1024, tn=1024):
    m, d = x.shape
    n = w_bf16.shape[1]
    tm = math.gcd(m, tm)
    return pl.pallas_call(
        _proj_kernel,
        grid=(m // tm, n // tn),
        in_specs=[
            pl.BlockSpec((tm, d), lambda i, j: (i, 0)),
            pl.BlockSpec((1, d), lambda i, j: (0, 0)),
            pl.BlockSpec((d, tn), lambda i, j: (0, j)),
        ],
        out_specs=pl.BlockSpec((tm, tn), lambda i, j: (i, j)),
        out_shape=jax.ShapeDtypeStruct((m, n), F32),
        scratch_shapes=[pltpu.VMEM((tm, d), BF16)],
        compiler_params=_params("parallel", "arbitrary"),
        name="norm_project",
    )(x, gain.reshape(1, d), w_bf16)


def _topk_select(sc, allowed, nblk):
    col = lax.broadcasted_iota(jnp.int32, sc.shape, 1)
    sc = jnp.where(allowed, sc, -jnp.inf)
    rank = jnp.zeros(sc.shape, jnp.int32)
    for i in range(nblk):
        si = sc[:, i:i + 1]
        beats = (si > sc) | ((si == sc) & (col > i))
        rank = rank + beats.astype(jnp.int32)
    return (rank < MOBA_TOPK) & allowed


def _moba_prompt_kernel(q_ref, k_ref, v_ref, o_ref, means_ref, bias_ref, *, nblk):
    qi = pl.program_id(2)
    blk = MOBA_BLOCK
    scale = D_HEAD ** -0.5

    @pl.when(qi == 0)
    def _():
        for j in range(nblk):
            means_ref[j:j + 1, :] = jnp.sum(k_ref[j * blk:(j + 1) * blk, :], axis=0, keepdims=True) * (1.0 / blk)

    q = q_ref[...]
    sc = _dot_t(q, means_ref[...], precision=HIGHEST)
    col = lax.broadcasted_iota(jnp.int32, sc.shape, 1)
    sel = _topk_select(sc, col < qi, nblk)
    for j in range(nblk):
        bias_ref[j] = jnp.broadcast_to(jnp.where(sel[:, j:j + 1], 0.0, NEG), (blk, LANE))

    qs = (q * scale).astype(BF16)
    d0 = pl.multiple_of(qi * blk, blk)
    kd = k_ref[pl.ds(d0, blk), :].astype(BF16)
    vd = v_ref[pl.ds(d0, blk), :].astype(BF16)
    s = _dot_t(qs, kd)
    r_i = lax.broadcasted_iota(jnp.int32, s.shape, 0)
    c_i = lax.broadcasted_iota(jnp.int32, s.shape, 1)
    s = jnp.where(c_i <= r_i, s, NEG)
    m = jnp.max(s, axis=-1, keepdims=True)
    p = jnp.exp(s - m)
    l = jnp.sum(p, axis=-1, keepdims=True)
    acc = _dot(p.astype(BF16), vd)

    def body(t, carry):
        m, l, acc = carry
        j = 2 * t
        j0 = pl.multiple_of(j * blk, 2 * blk)
        kj = k_ref[pl.ds(j0, 2 * blk), :].astype(BF16)
        vj = v_ref[pl.ds(j0, 2 * blk), :].astype(BF16)
        bias = jnp.concatenate([bias_ref[j]] * (blk // LANE) + [bias_ref[j + 1]] * (blk // LANE), axis=-1)
        s = _dot_t(qs, kj) + bias
        m_new = jnp.maximum(m, jnp.max(s, axis=-1, keepdims=True))
        alpha = jnp.exp(m - m_new)
        p = jnp.exp(s - m_new)
        l = alpha * l + jnp.sum(p, axis=-1, keepdims=True)
        acc = alpha * acc + _dot(p.astype(BF16), vj)
        return m_new, l, acc

    m, l, acc = lax.fori_loop(0, (qi + 1) // 2, body, (m, l, acc))
    o_ref[...] = acc / l


def moba_prompt_attn(z, n_seq, seq_len):
    blk = MOBA_BLOCK
    nblk = seq_len // blk
    return pl.pallas_call(
        functools.partial(_moba_prompt_kernel, nblk=nblk),
        grid=(n_seq, H_A, nblk),
        in_specs=[
            pl.BlockSpec((blk, D_HEAD), lambda n, h, i: (n * nblk + i, CB_QA + h)),
            pl.BlockSpec((seq_len, D_HEAD), lambda n, h, i: (n, CB_KA + h)),
            pl.BlockSpec((seq_len, D_HEAD), lambda n, h, i: (n, CB_VA + h)),
        ],
        out_specs=pl.BlockSpec((blk, D_HEAD), lambda n, h, i: (n * nblk + i, h)),
        out_shape=jax.ShapeDtypeStruct((n_seq * seq_len, W_A), F32),
        scratch_shapes=[pltpu.VMEM((nblk, D_HEAD), F32), pltpu.VMEM((nblk, blk, LANE), F32)],
        compiler_params=_params("parallel", "parallel", "arbitrary"),
        name="moba_prompt",
    )(z, z, z)


def _sb_block(zl, vj, c, upper, mask):
    sp = _softplus(zl)
    lk = -sp
    if mask is not None:
        lk = jnp.where(mask, lk, 0.0)
    hi, lo = _split_bf16(lk)
    after = _dot(hi, upper) + _dot(lo, upper)
    a = jnp.exp((zl - sp) + after + c)
    if mask is not None:
        a = jnp.where(mask, a, 0.0)
    return _dot(a.astype(BF16), vj), c + after[:, 0:1] + lk[:, 0:1]


def _upper(n):
    r_i = lax.broadcasted_iota(jnp.int32, (n, n), 0)
    c_i = lax.broadcasted_iota(jnp.int32, (n, n), 1)
    return jnp.where(r_i > c_i, 1.0, 0.0).astype(BF16), c_i < r_i


def _sb_prompt_kernel(q_ref, k_ref, v_ref, o_ref, *, blk):
    qi = pl.program_id(2)
    scale = D_HEAD ** -0.5
    qs = (q_ref[...] * scale).astype(BF16)
    upper, causal = _upper(blk)

    def block(j, c, acc, mask):
        j0 = pl.multiple_of(j * blk, blk)
        kj = k_ref[pl.ds(j0, blk), :].astype(BF16)
        vj = v_ref[pl.ds(j0, blk), :].astype(BF16)
        d, c = _sb_block(_dot_t(qs, kj), vj, c, upper, mask)
        return c, acc + d

    c, acc = block(qi, jnp.zeros((blk, 1), F32), jnp.zeros((blk, D_HEAD), F32), causal)

    def cond(carry):
        t, c, _ = carry
        return jnp.logical_and(t < qi, jnp.max(c) > SB_LOG_FLOOR)

    def body(carry):
        t, c, acc = carry
        c, acc = block(qi - 1 - t, c, acc, None)
        return t + 1, c, acc

    _, c, acc = lax.while_loop(cond, body, (jnp.int32(0), c, acc))
    o_ref[...] = acc


def sb_prompt_attn(z, n_seq, seq_len, *, blk=256):
    nblk = seq_len // blk
    return pl.pallas_call(
        functools.partial(_sb_prompt_kernel, blk=blk),
        grid=(n_seq, H_B, nblk),
        in_specs=[
            pl.BlockSpec((blk, D_HEAD), lambda n, h, i: (n * nblk + i, CB_QB + h)),
            pl.BlockSpec((seq_len, D_HEAD), lambda n, h, i: (n, CB_KB + h)),
            pl.BlockSpec((seq_len, D_HEAD), lambda n, h, i: (n, CB_VB + h)),
        ],
        out_specs=pl.BlockSpec((blk, D_HEAD), lambda n, h, i: (n * nblk + i, h)),
        out_shape=jax.ShapeDtypeStruct((n_seq * seq_len, W_B), F32),
        compiler_params=_params("parallel", "parallel", "arbitrary"),
        name="sb_prompt",
    )(z, z, z)


def _block_diag_rows(q, n_heads):
    col_head = lax.broadcasted_iota(jnp.int32, q.shape, 1) // D_HEAD
    return jnp.concatenate([jnp.where(col_head == h, q, 0.0) for h in range(n_heads)], axis=0)


def _pad_rows(x, rows):
    return jnp.concatenate([x, jnp.zeros((rows - x.shape[0], x.shape[1]), x.dtype)], axis=0)


def _head_diag(acc, n_heads, t):
    return jnp.concatenate(
        [acc[h * t:(h + 1) * t, h * D_HEAD:(h + 1) * D_HEAD] for h in range(n_heads)], axis=-1)


def _moba_sample_kernel(pt_ref, q_ref, kn_ref, vn_ref, *refs, t, n_pages, pg):
    k_refs = refs[:pg]
    v_refs = refs[pg:2 * pg]
    o_ref = refs[2 * pg]
    qf_ref, qb_ref, s_ref, p_ref, ksum_ref, acc_ref, l_ref = refs[2 * pg + 1:]
    del pt_ref
    s = pl.program_id(1)
    ng = n_pages // pg
    ppb = MOBA_BLOCK // PAGE_SIZE
    nblk = n_pages // ppb
    scale = D_HEAD ** -0.5

    @pl.when(s == 0)
    def _():
        qbd = _block_diag_rows(q_ref[...], H_A)
        qf_ref[...] = qbd
        qb_ref[...] = (qbd * scale).astype(BF16)

    @pl.when(s < ng)
    def _():
        for i in range(pg):
            page = s * pg + i
            kp = k_refs[i][0]
            ksum_ref[page] = jnp.sum(kp, axis=0, keepdims=True)
            s_ref[page] = _dot_t(qb_ref[...], kp.astype(BF16))

    @pl.when(s == ng - 1)
    def _():
        means = jnp.concatenate(
            [sum(ksum_ref[j * ppb + i] for i in range(ppb)) for j in range(nblk)], axis=0) * (1.0 / MOBA_BLOCK)
        sc = _dot_t(qf_ref[...], means, precision=HIGHEST)
        sel = _topk_select(sc, jnp.full(sc.shape, True), nblk)
        lo = _dot_t(qb_ref[...], _pad_rows(kn_ref[...], LANE).astype(BF16))
        r_t = lax.broadcasted_iota(jnp.int32, lo.shape, 0) % t
        c_j = lax.broadcasted_iota(jnp.int32, lo.shape, 1)
        lo = jnp.where(c_j <= r_t, lo, NEG)
        m = jnp.max(lo, axis=-1, keepdims=True)
        bias = [jnp.where(sel[:, j:j + 1], 0.0, NEG) for j in range(nblk)]
        for pi in range(n_pages):
            m = jnp.maximum(m, jnp.max(s_ref[pi] + bias[pi // ppb], axis=-1, keepdims=True))
        po = jnp.exp(lo - m)
        l = jnp.sum(po, axis=-1, keepdims=True)
        for pi in range(n_pages):
            pj = jnp.exp(s_ref[pi] + bias[pi // ppb] - m)
            l = l + jnp.sum(pj, axis=-1, keepdims=True)
            p_ref[pi] = pj.astype(BF16)
        l_ref[...] = jnp.broadcast_to(l, l_ref.shape)
        acc_ref[...] = _dot(po.astype(BF16), _pad_rows(vn_ref[...], LANE).astype(BF16))

    @pl.when(s >= ng)
    def _():
        for i in range(pg):
            page = (s - ng) * pg + i
            acc_ref[...] += _dot(p_ref[page], v_refs[i][0].astype(BF16))

    @pl.when(s == 2 * ng - 1)
    def _():
        o_ref[...] = _head_diag(acc_ref[...] / l_ref[:, 0:1], H_A, t)


def moba_sample_attn(z, pool_k, pool_v, page_table, row0, n_dec, t, *, pg=8):
    n_pages = page_table.shape[1]
    ng = n_pages // pg
    rb0 = row0 // t
    rows = H_A * t
    zspec = lambda cb: pl.BlockSpec((t, W_A), lambda b, s, pt: (rb0 + b, cb * LANE // W_A))
    kspec = lambda i: pl.BlockSpec(
        (1, PAGE_SIZE, W_A), lambda b, s, pt: (pt[b, jnp.minimum(s, ng - 1) * pg + i], 0, 0))
    vspec = lambda i: pl.BlockSpec(
        (1, PAGE_SIZE, W_A), lambda b, s, pt: (pt[b, jnp.maximum(s - ng, 0) * pg + i], 0, 0))
    grid_spec = pltpu.PrefetchScalarGridSpec(
        num_scalar_prefetch=1,
        grid=(n_dec, 2 * ng),
        in_specs=[zspec(CB_QA), zspec(CB_KA), zspec(CB_VA)] + [kspec(i) for i in range(pg)] + [vspec(i) for i in range(pg)],
        out_specs=pl.BlockSpec((t, W_A), lambda b, s, pt: (b, 0)),
        scratch_shapes=[
            pltpu.VMEM((rows, W_A), F32), pltpu.VMEM((rows, W_A), BF16),
            pltpu.VMEM((n_pages, rows, PAGE_SIZE), F32), pltpu.VMEM((n_pages, rows, PAGE_SIZE), BF16),
            pltpu.VMEM((n_pages, 1, W_A), F32),
            pltpu.VMEM((rows, W_A), F32), pltpu.VMEM((rows, LANE), F32)],
    )
    return pl.pallas_call(
        functools.partial(_moba_sample_kernel, t=t, n_pages=n_pages, pg=pg),
        grid_spec=grid_spec,
        out_shape=jax.ShapeDtypeStruct((n_dec * t, W_A), F32),
        compiler_params=_params("parallel", "arbitrary"),
        name="moba_sample",
    )(page_table, z, z, z, *([pool_k] * pg), *([pool_v] * pg))


def _sb_sample_kernel(pt_ref, q_ref, kn_ref, vn_ref, *refs, t, pg):
    k_refs = refs[:pg]
    v_refs = refs[pg:2 * pg]
    o_ref = refs[2 * pg]
    qb_ref, acc_ref, c_ref = refs[2 * pg + 1:]
    del pt_ref
    s = pl.program_id(1)
    scale = D_HEAD ** -0.5
    upper, causal = _upper(PAGE_SIZE)

    @pl.when(s == 0)
    def _():
        qb = (_block_diag_rows(q_ref[...], H_B) * scale).astype(BF16)
        qb_ref[...] = qb
        zl = _dot_t(qb, _pad_rows(kn_ref[...], PAGE_SIZE).astype(BF16))
        r_t = lax.broadcasted_iota(jnp.int32, zl.shape, 0) % t
        c_j = lax.broadcasted_iota(jnp.int32, zl.shape, 1)
        d, c = _sb_block(zl, _pad_rows(vn_ref[...], PAGE_SIZE).astype(BF16),
                         jnp.zeros((zl.shape[0], 1), F32), upper, c_j < r_t)
        acc_ref[...] = d
        c_ref[...] = jnp.broadcast_to(c, c_ref.shape)

    acc = acc_ref[...]
    c = c_ref[:, 0:1]
    for i in range(pg):
        d, c = _sb_block(_dot_t(qb_ref[...], k_refs[i][0].astype(BF16)), v_refs[i][0].astype(BF16), c, upper, None)
        acc = acc + d
    acc_ref[...] = acc
    c_ref[...] = jnp.broadcast_to(c, c_ref.shape)

    @pl.when(s == pl.num_programs(1) - 1)
    def _():
        o_ref[...] = _head_diag(acc, H_B, t)


def sb_sample_attn(z, pool_k, pool_v, page_table, row0, n_dec, t, *, pg=8):
    n_pages = page_table.shape[1]
    ng = n_pages // pg
    rb0 = row0 // t
    rows = H_B * t
    zspec = lambda cb: pl.BlockSpec((t, W_B), lambda b, s, pt: (rb0 + b, cb * LANE // W_B))
    pspec = lambda i: pl.BlockSpec(
        (1, PAGE_SIZE, W_B), lambda b, s, pt: (pt[b, n_pages - 1 - (s * pg + i)], 0, 0))
    grid_spec = pltpu.PrefetchScalarGridSpec(
        num_scalar_prefetch=1,
        grid=(n_dec, ng),
        in_specs=[zspec(CB_QB), zspec(CB_KB), zspec(CB_VB)] + [pspec(i) for i in range(pg)] * 2,
        out_specs=pl.BlockSpec((t, W_B), lambda b, s, pt: (b, 0)),
        scratch_shapes=[pltpu.VMEM((rows, W_B), BF16), pltpu.VMEM((rows, W_B), F32), pltpu.VMEM((rows, LANE), F32)],
    )
    return pl.pallas_call(
        functools.partial(_sb_sample_kernel, t=t, pg=pg),
        grid_spec=grid_spec,
        out_shape=jax.ShapeDtypeStruct((n_dec * t, W_B), F32),
        compiler_params=_params("parallel", "arbitrary"),
        name="sb_sample",
    )(page_table, z, z, z, *([pool_k] * pg), *([pool_v] * pg))


def _gla_kernel(q_ref, k_ref, v_ref, rc_ref, w2_ref, b_ref, s0_ref, o_ref, sfin_ref, st_ref, *, c, nchunk):
    step = pl.program_id(1)

    @pl.when(step == 0)
    def _():
        st_ref[...] = jnp.zeros_like(st_ref)
        for h in range(H_C):
            st_ref[h * DK_C:(h + 1) * DK_C, h * DV_C:(h + 1) * DV_C] = s0_ref[0, h]

    r_i = lax.broadcasted_iota(jnp.int32, (c, c), 0)
    c_i = lax.broadcasted_iota(jnp.int32, (c, c), 1)
    tril = jnp.where(c_i <= r_i, 1.0, 0.0).astype(F32)
    s_idx = lax.broadcasted_iota(jnp.int32, (c, WK_C), 0)
    kh = lax.broadcasted_iota(jnp.int32, (WK_C, WV_C), 0) // DK_C
    vh = lax.broadcasted_iota(jnp.int32, (WK_C, WV_C), 1) // DV_C
    diag = kh == vh
    expand = jnp.where(diag, 1.0, 0.0).astype(BF16)
    pad_row = lax.broadcasted_iota(jnp.int32, (LANE, WK_C), 0)
    pad_col = lax.broadcasted_iota(jnp.int32, (WK_C, LANE), 1)

    def chunk(ci, carry):
        r0 = pl.multiple_of(ci * c, c)
        q = q_ref[pl.ds(r0, c), :] * (DK_C ** -0.5)
        k = k_ref[pl.ds(r0, c), :]
        v = v_ref[pl.ds(r0, c), :]
        x = _dot(rc_ref[pl.ds(r0, c), :].astype(BF16), w2_ref[...]) + b_ref[...]
        la = (jnp.minimum(x, 0.0) - jnp.log(1.0 + jnp.exp(-jnp.abs(x)))) * (1.0 / GLA_GATE_NORM)
        bc = _dot(tril, la, precision=HIGHEST)
        ps = []
        for t in range(c):
            diff = bc[t:t + 1, :] - bc
            dec = jnp.exp(jnp.where(s_idx <= t, diff, -jnp.inf))
            ps.append((q[t:t + 1, :] * dec) * k)
        a = _dot(jnp.concatenate(ps, axis=0).astype(BF16), expand)
        o_rows = [jnp.sum(a[t * c:(t + 1) * c, :] * v, axis=0, keepdims=True) for t in range(c)]
        st = st_ref[...]
        o = jnp.concatenate(o_rows, axis=0) + _dot((q * jnp.exp(bc)).astype(BF16), st.astype(BF16))
        o_ref[pl.ds(r0, c), :] = o
        b_last = bc[c - 1:c, :]
        kd = k * jnp.exp(b_last - bc)
        packed = jnp.where(pad_row == c, jnp.broadcast_to(b_last, (LANE, WK_C)), _pad_rows(kd, LANE))
        packed_t = packed.T
        g_col = jnp.exp(packed_t[:, c:c + 1])
        kd_t = jnp.where(pad_col < c, packed_t, 0.0).astype(BF16)
        ds = _dot(kd_t, _pad_rows(v, LANE).astype(BF16))
        st_ref[...] = st * g_col + jnp.where(diag, ds, 0.0)
        return carry

    lax.fori_loop(0, nchunk, chunk, 0)

    @pl.when(step == pl.num_programs(1) - 1)
    def _():
        for h in range(H_C):
            sfin_ref[0, h] = st_ref[h * DK_C:(h + 1) * DK_C, h * DV_C:(h + 1) * DV_C]


def _gla_call(z, w2, b, s0, row0, n, length, c, tb):
    steps = length // tb
    rb0 = row0 // tb
    row = lambda i, s: rb0 + i * steps + s
    w2p = jnp.concatenate([w2, jnp.zeros((LANE - GLA_RANK, WK_C), w2.dtype)], axis=0).astype(BF16)
    return pl.pallas_call(
        functools.partial(_gla_kernel, c=c, nchunk=tb // c),
        grid=(n, steps),
        in_specs=[
            pl.BlockSpec((tb, WK_C), lambda i, s: (row(i, s), CB_QC * LANE // WK_C)),
            pl.BlockSpec((tb, WK_C), lambda i, s: (row(i, s), CB_KC * LANE // WK_C)),
            pl.BlockSpec((tb, WV_C), lambda i, s: (row(i, s), CB_VC * LANE // WV_C)),
            pl.BlockSpec((tb, LANE), lambda i, s: (row(i, s), CB_RC)),
            pl.BlockSpec((LANE, WK_C), lambda i, s: (0, 0)),
            pl.BlockSpec((1, WK_C), lambda i, s: (0, 0)),
            pl.BlockSpec((1, H_C, DK_C, DV_C), lambda i, s: (i, 0, 0, 0)),
        ],
        out_specs=[
            pl.BlockSpec((tb, WV_C), lambda i, s: (i * steps + s, 0)),
            pl.BlockSpec((1, H_C, DK_C, DV_C), lambda i, s: (i, 0, 0, 0)),
        ],
        out_shape=[jax.ShapeDtypeStruct((n * length, WV_C), F32), jax.ShapeDtypeStruct((n, H_C, DK_C, DV_C), F32)],
        scratch_shapes=[pltpu.VMEM((WK_C, WV_C), F32)],
        compiler_params=_params("parallel", "arbitrary"),
        name="gla",
    )(z, z, z, z, w2p, b.reshape(1, WK_C), s0)


def gla_prompt(z, w2, b, n_seq, seq_len, *, tb=256):
    s0 = jnp.zeros((n_seq, H_C, DK_C, DV_C), F32)
    return _gla_call(z, w2, b, s0, 0, n_seq, seq_len, math.gcd(seq_len, GLA_CHUNK), tb)


def gla_sample(z, w2, b, s0, row0, n_dec, t):
    return _gla_call(z, w2, b, s0.astype(F32), row0, n_dec, t, math.gcd(t, GLA_CHUNK), t)


def _branch_kernel(ma_ref, mb_ref, mc_ref, ga_ref, gb_ref, gc_ref, oa_ref, ob_ref, oc_ref, gain_ref,
                   wa_ref, wb_ref, wc_ref, m_ref):
    silu = lambda g: g * _sigmoid(g)
    ya = _dot((oa_ref[...] * silu(ga_ref[...])).astype(BF16), wa_ref[...])
    yb = _dot((ob_ref[...] * silu(gb_ref[...])).astype(BF16), wb_ref[...])
    oc = oc_ref[...]
    gain = gain_ref[...]
    heads = []
    for h in range(H_C):
        oh = oc[:, h * DV_C:(h + 1) * DV_C]
        heads.append(oh * lax.rsqrt(jnp.mean(oh * oh, axis=-1, keepdims=True) + RMS_EPS) * gain)
    ocn = jnp.concatenate(heads, axis=-1)
    yc = _dot((ocn * silu(gc_ref[...])).astype(BF16), wc_ref[...])
    m = _sigmoid(ma_ref[...]) * ya + _sigmoid(mb_ref[...]) * yb + _sigmoid(mc_ref[...]) * yc
    m_ref[...] = m.astype(BF16)


def branch_merge(z, oa, ob, oc, gla_gain, wa, wb, wc, *, tm=256):
    m = z.shape[0]
    tm = math.gcd(m, tm)
    zb = lambda cb, w: pl.BlockSpec((tm, w), lambda i: (i, cb * LANE // w))
    full = lambda shape: pl.BlockSpec(shape, lambda i: (0,) * len(shape))
    rows = lambda w: pl.BlockSpec((tm, w), lambda i: (i, 0))
    return pl.pallas_call(
        _branch_kernel,
        grid=(m // tm,),
        in_specs=[zb(CB_MA, D_MODEL), zb(CB_MB, D_MODEL), zb(CB_MC, D_MODEL),
                  zb(CB_GA, W_A), zb(CB_GB, W_B), zb(CB_GC, WV_C),
                  rows(W_A), rows(W_B), rows(WV_C), full((1, DV_C)),
                  full((W_A, D_MODEL)), full((W_B, D_MODEL)), full((WV_C, D_MODEL))],
        out_specs=rows(D_MODEL),
        out_shape=jax.ShapeDtypeStruct((m, D_MODEL), BF16),
        compiler_params=_params("parallel"),
        name="branch_merge",
    )(z, z, z, z, z, z, oa, ob, oc, gla_gain.reshape(1, DV_C), wa, wb, wc)


def _outproj_kernel(x_ref, m_ref, w_ref, g_ref, o_ref, *, final):
    y = x_ref[...] + _dot(m_ref[...], w_ref[...])
    if final:
        y = y * lax.rsqrt(jnp.mean(y * y, axis=-1, keepdims=True) + RMS_EPS) * g_ref[...]
    o_ref[...] = y


def out_project(x, m, w_out, final_gain, *, final, tm=512):
    rows, d = x.shape
    tm = math.gcd(rows, tm)
    return pl.pallas_call(
        functools.partial(_outproj_kernel, final=final),
        grid=(rows // tm,),
        in_specs=[pl.BlockSpec((tm, d), lambda i: (i, 0)), pl.BlockSpec((tm, d), lambda i: (i, 0)),
                  pl.BlockSpec((d, d), lambda i: (0, 0)), pl.BlockSpec((1, d), lambda i: (0, 0))],
        out_specs=pl.BlockSpec((tm, d), lambda i: (i, 0)),
        out_shape=jax.ShapeDtypeStruct((rows, d), F32),
        compiler_params=_params("parallel"),
        name="out_project",
    )(x, m, w_out, final_gain.reshape(1, d))


def _cols(z, cb, width):
    return z[:, cb * LANE: cb * LANE + width]


def kernel(x_prompt, x_sample, cache_moba_k, cache_moba_v, cache_sb_k, cache_sb_v, state_gla, page_table,
           norm_gain, w_in, gla_w2, gla_b, gla_out_gain, w_br_a, w_br_b, w_br_c, w_out, final_gain):
    n_seq, seq_len, _ = x_prompt.shape
    n_dec, dec_len, _ = x_sample.shape
    mp = n_seq * seq_len
    ms = n_dec * dec_len
    n_pool = cache_moba_k.shape[1]
    x_all = jnp.concatenate([x_prompt.reshape(mp, D_MODEL), x_sample.reshape(ms, D_MODEL)], axis=0)
    w_all = jnp.concatenate(
        [w_in[:, :, MAIN_W + GLA_RANK:], w_in[:, :, :MAIN_W], w_in[:, :, MAIN_W:MAIN_W + GLA_RANK],
         jnp.zeros((DEPTH, D_MODEL, RC_PAD - GLA_RANK), w_in.dtype)], axis=-1).astype(BF16)
    wa, wb, wc, wo = (w.astype(BF16) for w in (w_br_a, w_br_b, w_br_c, w_out))
    pool_ak = cache_moba_k.reshape(DEPTH, n_pool, PAGE_SIZE, W_A)
    pool_av = cache_moba_v.reshape(DEPTH, n_pool, PAGE_SIZE, W_A)
    pool_bk = cache_sb_k.reshape(DEPTH, n_pool, PAGE_SIZE, W_B)
    pool_bv = cache_sb_v.reshape(DEPTH, n_pool, PAGE_SIZE, W_B)

    kv_p = [[] for _ in range(4)]
    kv_s = [[] for _ in range(4)]
    st_p, st_s = [], []
    for l in range(DEPTH):
        z = norm_project(x_all, norm_gain[l], w_all[l])
        oa_p = moba_prompt_attn(z, n_seq, seq_len)
        ob_p = sb_prompt_attn(z, n_seq, seq_len)
        oc_p, sf_p = gla_prompt(z, gla_w2[l], gla_b[l], n_seq, seq_len)
        oa_s = moba_sample_attn(z, pool_ak[l], pool_av[l], page_table, mp, n_dec, dec_len)
        ob_s = sb_sample_attn(z, pool_bk[l], pool_bv[l], page_table, mp, n_dec, dec_len)
        oc_s, sf_s = gla_sample(z, gla_w2[l], gla_b[l], state_gla[l], mp, n_dec, dec_len)
        oa = jnp.concatenate([oa_p, oa_s], axis=0)
        ob = jnp.concatenate([ob_p, ob_s], axis=0)
        oc = jnp.concatenate([oc_p, oc_s], axis=0)
        m = branch_merge(z, oa, ob, oc, gla_out_gain[l], wa[l], wb[l], wc[l])
        x_all = out_project(x_all, m, wo[l], final_gain, final=(l == DEPTH - 1))
        for dst, cb, nh in zip(range(4), (CB_KA, CB_VA, CB_KB, CB_VB), (H_A, H_A, H_B, H_B)):
            seg = _cols(z, cb, nh * D_HEAD)
            kv_p[dst].append(seg[:mp].reshape(n_seq, seq_len, nh, D_HEAD))
            kv_s[dst].append(seg[mp:].reshape(n_dec, dec_len, nh, D_HEAD))
        st_p.append(sf_p)
        st_s.append(sf_s)

    return (x_all[:mp].reshape(n_seq, seq_len, D_MODEL), x_all[mp:].reshape(n_dec, dec_len, D_MODEL),
            jnp.stack(kv_p[0]), jnp.stack(kv_p[1]), jnp.stack(kv_p[2]), jnp.stack(kv_p[3]), jnp.stack(st_p),
            jnp.stack(kv_s[0]), jnp.stack(kv_s[1]), jnp.stack(kv_s[2]), jnp.stack(kv_s[3]), jnp.stack(st_s))
```

```python
import functools
import math

import jax
import jax.numpy as jnp
from jax import lax
from jax.experimental import pallas as pl
from jax.experimental.pallas import tpu as pltpu

D_MODEL = 2048
DEPTH = 2
PAGE_SIZE = 128
D_HEAD = 128
H_A = 8
H_B = 4
H_C = 4
DK_C = 64
DV_C = 128
W_A = H_A * D_HEAD
W_B = H_B * D_HEAD
WK_C = H_C * DK_C
WV_C = H_C * DV_C
GLA_RANK = 16
GLA_GATE_NORM = 16.0
GLA_CHUNK = 16
MOBA_BLOCK = 256
MOBA_TOPK = 3
RMS_EPS = 1e-6

LANE = 128
MAIN_W = 4 * W_A + 4 * W_B + 2 * WK_C + 2 * WV_C
GATE_W = 3 * D_MODEL
RC_PAD = 512
Z_W = GATE_W + MAIN_W + RC_PAD
CB_MA, CB_MB, CB_MC = 0, 16, 32
CB_QA, CB_KA, CB_VA, CB_GA = 48, 56, 64, 72
CB_QB, CB_KB, CB_VB, CB_GB = 80, 84, 88, 92
CB_QC, CB_KC, CB_VC, CB_GC = 96, 98, 100, 104
CB_RC = 108

NEG = -1e30
SB_LOG_FLOOR = -104.0
VMEM_LIMIT = 48 * 1024 * 1024
MOBA_HEADS_PER_STEP = 2

F32 = jnp.float32
BF16 = jnp.bfloat16
HIGHEST = lax.Precision.HIGHEST


def _dot_t(a, b, **kw):
    return lax.dot_general(a, b, (((1,), (1,)), ((), ())), preferred_element_type=F32, **kw)


def _dot(a, b, **kw):
    return jnp.dot(a, b, preferred_element_type=F32, **kw)


def _softplus(z):
    return jnp.maximum(z, 0.0) + jnp.log(1.0 + jnp.exp(-jnp.abs(z)))


def _sigmoid(z):
    return 1.0 / (1.0 + jnp.exp(-z))


def _split_bf16(x):
    hi = x.astype(BF16)
    lo = (x - hi.astype(F32)).astype(BF16)
    return hi, lo


def _params(*sem):
    return pltpu.CompilerParams(dimension_semantics=sem, vmem_limit_bytes=VMEM_LIMIT)


def _pad_rows(x, rows):
    return jnp.concatenate([x, jnp.zeros((rows - x.shape[0], x.shape[1]), x.dtype)], axis=0)


def _head_cols(x, h):
    return x[:, h * D_HEAD:(h + 1) * D_HEAD]


def _proj_kernel(x_ref, g_ref, w_ref, o_ref, xn_ref):
    @pl.when(pl.program_id(1) == 0)
    def _():
        x = x_ref[...]
        ms = jnp.mean(x * x, axis=-1, keepdims=True)
        xn_ref[...] = (x * lax.rsqrt(ms + RMS_EPS) * g_ref[...]).astype(BF16)

    o_ref[...] = _dot(xn_ref[...], w_ref[...])


def norm_project(x, gain, w_bf16, layer, *, tm=1024, tn=1024):
    m, d = x.shape
    n = w_bf16.shape[2]
    tm = math.gcd(m, tm)
    return pl.pallas_call(
        _proj_kernel,
        grid=(m // tm, n // tn),
        in_specs=[
            pl.BlockSpec((tm, d), lambda i, j: (i, 0)),
            pl.BlockSpec((1, d), lambda i, j: (0, 0)),
            pl.BlockSpec((None, d, tn), lambda i, j: (layer, 0, j)),
        ],
        out_specs=pl.BlockSpec((tm, tn), lambda i, j: (i, j)),
        out_shape=jax.ShapeDtypeStruct((m, n), F32),
        scratch_shapes=[pltpu.VMEM((tm, d), BF16)],
        compiler_params=_params("parallel", "arbitrary"),
        name="norm_project",
    )(x, gain.reshape(1, d), w_bf16)


def _topk_select(sc, allowed, nblk):
    col = lax.broadcasted_iota(jnp.int32, sc.shape, 1)
    sc = jnp.where(allowed, sc, -jnp.inf)
    rank = jnp.zeros(sc.shape, jnp.int32)
    for i in range(nblk):
        si = sc[:, i:i + 1]
        beats = (si > sc) | ((si == sc) & (col > i))
        rank = rank + beats.astype(jnp.int32)
    return (rank < MOBA_TOPK) & allowed


def _moba_prompt_kernel(q_ref, k_ref, v_ref, o_ref, means_ref, bias_ref, *, nblk, hp):
    qi = pl.program_id(2)
    blk = MOBA_BLOCK
    scale = D_HEAD ** -0.5
    hcols = lambda h: slice(h * D_HEAD, (h + 1) * D_HEAD)

    @pl.when(qi == 0)
    def _():
        for h in range(hp):
            for j in range(nblk):
                means_ref[h, j:j + 1, :] = (
                    jnp.sum(k_ref[j * blk:(j + 1) * blk, hcols(h)], axis=0, keepdims=True) * (1.0 / blk))

    d0 = pl.multiple_of(qi * blk, blk)
    r_i = lax.broadcasted_iota(jnp.int32, (blk, blk), 0)
    c_i = lax.broadcasted_iota(jnp.int32, (blk, blk), 1)
    qss, init = [], []
    for h in range(hp):
        q = q_ref[:, hcols(h)]
        sc = _dot_t(q, means_ref[h], precision=HIGHEST)
        col = lax.broadcasted_iota(jnp.int32, sc.shape, 1)
        sel = _topk_select(sc, col < qi, nblk)
        for j in range(nblk):
            bias_ref[h, j] = jnp.broadcast_to(jnp.where(sel[:, j:j + 1], 0.0, NEG), (blk, LANE))
        qs = (q * scale).astype(BF16)
        s = _dot_t(qs, k_ref[pl.ds(d0, blk), hcols(h)].astype(BF16))
        s = jnp.where(c_i <= r_i, s, NEG)
        m = jnp.max(s, axis=-1, keepdims=True)
        p = jnp.exp(s - m)
        l = jnp.sum(p, axis=-1, keepdims=True)
        acc = _dot(p.astype(BF16), v_ref[pl.ds(d0, blk), hcols(h)].astype(BF16))
        qss.append(qs)
        init.append((m, l, acc))

    def body(t, carry):
        j = 2 * t
        j0 = pl.multiple_of(j * blk, 2 * blk)
        out = []
        for h in range(hp):
            m, l, acc = carry[h]
            kj = k_ref[pl.ds(j0, 2 * blk), hcols(h)].astype(BF16)
            vj = v_ref[pl.ds(j0, 2 * blk), hcols(h)].astype(BF16)
            bias = jnp.concatenate(
                [bias_ref[h, j]] * (blk // LANE) + [bias_ref[h, j + 1]] * (blk // LANE), axis=-1)
            s = _dot_t(qss[h], kj) + bias
            m_new = jnp.maximum(m, jnp.max(s, axis=-1, keepdims=True))
            alpha = jnp.exp(m - m_new)
            p = jnp.exp(s - m_new)
            l = alpha * l + jnp.sum(p, axis=-1, keepdims=True)
            acc = alpha * acc + _dot(p.astype(BF16), vj)
            out.append((m_new, l, acc))
        return tuple(out)

    fin = lax.fori_loop(0, (qi + 1) // 2, body, tuple(init))
    o_ref[...] = jnp.concatenate([acc / l for (_, l, acc) in fin], axis=-1)


def moba_prompt_attn(z, n_seq, seq_len):
    blk = MOBA_BLOCK
    nblk = seq_len // blk
    hp = MOBA_HEADS_PER_STEP
    assert nblk % 2 == 0 and H_A % hp == 0
    w = hp * D_HEAD
    return pl.pallas_call(
        functools.partial(_moba_prompt_kernel, nblk=nblk, hp=hp),
        grid=(n_seq, H_A // hp, nblk),
        in_specs=[
            pl.BlockSpec((blk, w), lambda n, h, i: (n * nblk + i, CB_QA // hp + h)),
            pl.BlockSpec((seq_len, w), lambda n, h, i: (n, CB_KA // hp + h)),
            pl.BlockSpec((seq_len, w), lambda n, h, i: (n, CB_VA // hp + h)),
        ],
        out_specs=pl.BlockSpec((blk, w), lambda n, h, i: (n * nblk + i, h)),
        out_shape=jax.ShapeDtypeStruct((n_seq * seq_len, W_A), F32),
        scratch_shapes=[pltpu.VMEM((hp, nblk, D_HEAD), F32), pltpu.VMEM((hp, nblk, blk, LANE), F32)],
        compiler_params=_params("parallel", "parallel", "arbitrary"),
        name="moba_prompt",
    )(z, z, z)


def _sb_block(zl, c, upper, mask):
    sp = _softplus(zl)
    lk = -sp
    if mask is not None:
        lk = jnp.where(mask, lk, 0.0)
    hi, lo = _split_bf16(lk)
    after = _dot(hi, upper) + _dot(lo, upper)
    a = jnp.exp((zl - sp) + after + c)
    if mask is not None:
        a = jnp.where(mask, a, 0.0)
    return a, c + after[:, 0:1] + lk[:, 0:1]


def _upper(n):
    r_i = lax.broadcasted_iota(jnp.int32, (n, n), 0)
    c_i = lax.broadcasted_iota(jnp.int32, (n, n), 1)
    return jnp.where(r_i > c_i, 1.0, 0.0).astype(BF16), c_i < r_i


def _sb_prompt_kernel(q_ref, k_ref, v_ref, o_ref, *, blk):
    qi = pl.program_id(2)
    scale = D_HEAD ** -0.5
    qs = (q_ref[...] * scale).astype(BF16)
    upper, causal = _upper(blk)

    def block(j, c, acc, mask):
        j0 = pl.multiple_of(j * blk, blk)
        kj = k_ref[pl.ds(j0, blk), :].astype(BF16)
        vj = v_ref[pl.ds(j0, blk), :].astype(BF16)
        a, c = _sb_block(_dot_t(qs, kj), c, upper, mask)
        return c, acc + _dot(a.astype(BF16), vj)

    c, acc = block(qi, jnp.zeros((blk, 1), F32), jnp.zeros((blk, D_HEAD), F32), causal)

    def cond(carry):
        t, c, _ = carry
        return jnp.logical_and(t < qi, jnp.max(c) > SB_LOG_FLOOR)

    def body(carry):
        t, c, acc = carry
        c, acc = block(qi - 1 - t, c, acc, None)
        return t + 1, c, acc

    _, c, acc = lax.while_loop(cond, body, (jnp.int32(0), c, acc))
    o_ref[...] = acc


def sb_prompt_attn(z, n_seq, seq_len, *, blk=256):
    nblk = seq_len // blk
    return pl.pallas_call(
        functools.partial(_sb_prompt_kernel, blk=blk),
        grid=(n_seq, H_B, nblk),
        in_specs=[
            pl.BlockSpec((blk, D_HEAD), lambda n, h, i: (n * nblk + i, CB_QB + h)),
            pl.BlockSpec((seq_len, D_HEAD), lambda n, h, i: (n, CB_KB + h)),
            pl.BlockSpec((seq_len, D_HEAD), lambda n, h, i: (n, CB_VB + h)),
        ],
        out_specs=pl.BlockSpec((blk, D_HEAD), lambda n, h, i: (n * nblk + i, h)),
        out_shape=jax.ShapeDtypeStruct((n_seq * seq_len, W_B), F32),
        compiler_params=_params("parallel", "parallel", "arbitrary"),
        name="sb_prompt",
    )(z, z, z)


def _page_head(ref, h, n_heads):
    return ref[0, 0, pl.ds(h, PAGE_SIZE, stride=n_heads), :]


def _stack_heads(x, n_heads):
    return jnp.concatenate([_head_cols(x, h) for h in range(n_heads)], axis=0)


def _unstack_heads(x, n_heads, t):
    return jnp.concatenate([x[h * t:(h + 1) * t, :] for h in range(n_heads)], axis=-1)


def _moba_sample_kernel(pt_ref, q_ref, kn_ref, vn_ref, *refs, t, n_pages, pg):
    k_refs = refs[:pg]
    v_refs = refs[pg:2 * pg]
    o_ref = refs[2 * pg]
    s_ref, p_ref, ksum_ref, acc_ref, l_ref = refs[2 * pg + 1:]
    del pt_ref
    s = pl.program_id(1)
    ng = n_pages // pg
    ppb = MOBA_BLOCK // PAGE_SIZE
    nblk = n_pages // ppb
    bpg = pg // ppb
    scale = D_HEAD ** -0.5
    hrows = lambda h: slice(h * t, (h + 1) * t)
    qs = _stack_heads(q_ref[...], H_A) * scale
    q_head = lambda h: qs[hrows(h)].astype(BF16)

    @pl.when(s < ng)
    def _():
        for i in range(pg):
            ksum_ref[s * pg + i] = jnp.sum(k_refs[i][0, 0].reshape(PAGE_SIZE, H_A, D_HEAD), axis=0)
        for h in range(H_A):
            kh = jnp.concatenate([_page_head(k_refs[i], h, H_A) for i in range(pg)], axis=0).astype(BF16)
            s_ref[s, hrows(h), :] = _dot_t(q_head(h), kh)

    @pl.when(s == ng - 1)
    def _():
        qf = _stack_heads(q_ref[...], H_A)
        sc = []
        for j in range(nblk):
            mean_j = sum(ksum_ref[j * ppb + i] for i in range(ppb)) * (1.0 / MOBA_BLOCK)
            mexp = jnp.concatenate([jnp.broadcast_to(mean_j[h:h + 1, :], (t, D_HEAD)) for h in range(H_A)], axis=0)
            sc.append(jnp.sum(qf * mexp, axis=-1, keepdims=True))
        bias = []
        for j in range(nblk):
            rank = sum(((sc[i] >= sc[j]) if i < j else (sc[i] > sc[j])).astype(jnp.int32)
                       for i in range(nblk) if i != j)
            bias.append(jnp.where(rank < MOBA_TOPK, 0.0, NEG))
        kn = kn_ref[...]
        lo = jnp.concatenate(
            [_dot_t(q_head(h), _pad_rows(_head_cols(kn, h), LANE).astype(BF16)) for h in range(H_A)], axis=0)
        r_t = lax.broadcasted_iota(jnp.int32, lo.shape, 0) % t
        c_j = lax.broadcasted_iota(jnp.int32, lo.shape, 1)
        lo = jnp.where(c_j <= r_t, lo, NEG)
        m = jnp.max(lo, axis=-1, keepdims=True)
        masked = []
        for g in range(ng):
            bg = jnp.concatenate(
                [jnp.broadcast_to(bias[g * bpg + jj], (H_A * t, MOBA_BLOCK)) for jj in range(bpg)], axis=-1)
            sg = s_ref[g] + bg
            masked.append(sg)
            m = jnp.maximum(m, jnp.max(sg, axis=-1, keepdims=True))
        po = jnp.exp(lo - m)
        l = jnp.sum(po, axis=-1, keepdims=True)
        for g in range(ng):
            pgp = jnp.exp(masked[g] - m)
            l = l + jnp.sum(pgp, axis=-1, keepdims=True)
            p_ref[g] = pgp
        l_ref[...] = jnp.broadcast_to(l, l_ref.shape)
        vn = vn_ref[...]
        acc_ref[...] = jnp.concatenate(
            [_dot(po[hrows(h)].astype(BF16), _pad_rows(_head_cols(vn, h), LANE).astype(BF16)) for h in range(H_A)],
            axis=0)

    @pl.when(s >= ng)
    def _():
        for h in range(H_A):
            vh = jnp.concatenate([_page_head(v_refs[i], h, H_A) for i in range(pg)], axis=0).astype(BF16)
            acc_ref[hrows(h), :] += _dot(p_ref[s - ng, hrows(h), :].astype(BF16), vh)

    @pl.when(s == 2 * ng - 1)
    def _():
        o_ref[...] = _unstack_heads(acc_ref[...] / l_ref[:, 0:1], H_A, t)


def moba_sample_attn(z, pool_k, pool_v, page_table, layer, n_dec, t, *, pg=8):
    n_pages = page_table.shape[1]
    ng = n_pages // pg
    rows = H_A * t
    zspec = lambda cb: pl.BlockSpec((t, W_A), lambda b, s, pt: (b, cb * LANE // W_A))
    page_block = (1, 1, PAGE_SIZE * H_A, D_HEAD)
    kspec = lambda i: pl.BlockSpec(
        page_block, lambda b, s, pt: (layer, pt[b, jnp.minimum(s, ng - 1) * pg + i], 0, 0))
    vspec = lambda i: pl.BlockSpec(
        page_block, lambda b, s, pt: (layer, pt[b, jnp.maximum(s - ng, 0) * pg + i], 0, 0))
    grid_spec = pltpu.PrefetchScalarGridSpec(
        num_scalar_prefetch=1,
        grid=(n_dec, 2 * ng),
        in_specs=[zspec(CB_QA), zspec(CB_KA), zspec(CB_VA)] + [kspec(i) for i in range(pg)] + [vspec(i) for i in range(pg)],
        out_specs=pl.BlockSpec((t, W_A), lambda b, s, pt: (b, 0)),
        scratch_shapes=[
            pltpu.VMEM((ng, rows, pg * PAGE_SIZE), F32), pltpu.VMEM((ng, rows, pg * PAGE_SIZE), F32),
            pltpu.VMEM((n_pages, H_A, D_HEAD), F32),
            pltpu.VMEM((rows, D_HEAD), F32), pltpu.VMEM((rows, LANE), F32)],
    )
    return pl.pallas_call(
        functools.partial(_moba_sample_kernel, t=t, n_pages=n_pages, pg=pg),
        grid_spec=grid_spec,
        out_shape=jax.ShapeDtypeStruct((n_dec * t, W_A), F32),
        compiler_params=_params("parallel", "arbitrary"),
        name="moba_sample",
    )(page_table, z, z, z, *([pool_k] * pg), *([pool_v] * pg))


def _sb_sample_kernel(pt_ref, q_ref, kn_ref, vn_ref, *refs, t, pg):
    k_refs = refs[:pg]
    v_refs = refs[pg:2 * pg]
    o_ref = refs[2 * pg]
    acc_ref, c_ref = refs[2 * pg + 1:]
    del pt_ref
    s = pl.program_id(1)
    scale = D_HEAD ** -0.5
    upper, _ = _upper(PAGE_SIZE)
    hrows = lambda h: slice(h * t, (h + 1) * t)
    qs = _stack_heads(q_ref[...], H_B) * scale

    def visit(k_of, v_of, mask):
        zl = jnp.concatenate([_dot_t(qs[hrows(h)].astype(BF16), k_of(h)) for h in range(H_B)], axis=0)
        a, c = _sb_block(zl, c_ref[:, 0:1], upper, mask)
        acc_ref[...] += jnp.concatenate([_dot(a[hrows(h)].astype(BF16), v_of(h)) for h in range(H_B)], axis=0)
        c_ref[...] = jnp.broadcast_to(c, c_ref.shape)

    @pl.when(s == 0)
    def _():
        acc_ref[...] = jnp.zeros_like(acc_ref)
        c_ref[...] = jnp.zeros_like(c_ref)
        kn, vn = kn_ref[...], vn_ref[...]
        r_t = lax.broadcasted_iota(jnp.int32, (H_B * t, PAGE_SIZE), 0) % t
        c_j = lax.broadcasted_iota(jnp.int32, (H_B * t, PAGE_SIZE), 1)
        visit(lambda h: _pad_rows(_head_cols(kn, h), PAGE_SIZE).astype(BF16),
              lambda h: _pad_rows(_head_cols(vn, h), PAGE_SIZE).astype(BF16), c_j < r_t)

    for i in range(pg):
        @pl.when(jnp.max(c_ref[...]) > SB_LOG_FLOOR)
        def _(i=i):
            visit(lambda h: _page_head(k_refs[i], h, H_B).astype(BF16),
                  lambda h: _page_head(v_refs[i], h, H_B).astype(BF16), None)

    @pl.when(s == pl.num_programs(1) - 1)
    def _():
        o_ref[...] = _unstack_heads(acc_ref[...], H_B, t)


def sb_sample_attn(z, pool_k, pool_v, page_table, layer, n_dec, t, *, pg=8):
    n_pages = page_table.shape[1]
    ng = n_pages // pg
    rows = H_B * t
    zspec = lambda cb: pl.BlockSpec((t, W_B), lambda b, s, pt: (b, cb * LANE // W_B))
    pspec = lambda i: pl.BlockSpec(
        (1, 1, PAGE_SIZE * H_B, D_HEAD), lambda b, s, pt: (layer, pt[b, n_pages - 1 - (s * pg + i)], 0, 0))
    grid_spec = pltpu.PrefetchScalarGridSpec(
        num_scalar_prefetch=1,
        grid=(n_dec, ng),
        in_specs=[zspec(CB_QB), zspec(CB_KB), zspec(CB_VB)] + [pspec(i) for i in range(pg)] * 2,
        out_specs=pl.BlockSpec((t, W_B), lambda b, s, pt: (b, 0)),
        scratch_shapes=[pltpu.VMEM((rows, D_HEAD), F32), pltpu.VMEM((rows, LANE), F32)],
    )
    return pl.pallas_call(
        functools.partial(_sb_sample_kernel, t=t, pg=pg),
        grid_spec=grid_spec,
        out_shape=jax.ShapeDtypeStruct((n_dec * t, W_B), F32),
        compiler_params=_params("parallel", "arbitrary"),
        name="sb_sample",
    )(page_table, z, z, z, *([pool_k] * pg), *([pool_v] * pg))


def _gla_kernel(q_ref, k_ref, v_ref, rc_ref, w2_ref, b_ref, s0_ref, o_ref, sfin_ref, st_ref, *, c, nchunk):
    step = pl.program_id(1)

    @pl.when(step == 0)
    def _():
        st_ref[...] = jnp.zeros_like(st_ref)
        for h in range(H_C):
            st_ref[h * DK_C:(h + 1) * DK_C, h * DV_C:(h + 1) * DV_C] = s0_ref[0, h]

    r_i = lax.broadcasted_iota(jnp.int32, (c, c), 0)
    c_i = lax.broadcasted_iota(jnp.int32, (c, c), 1)
    tril = jnp.where(c_i <= r_i, 1.0, 0.0).astype(F32)
    s_idx = lax.broadcasted_iota(jnp.int32, (c, WK_C), 0)
    kh = lax.broadcasted_iota(jnp.int32, (WK_C, WV_C), 0) // DK_C
    vh = lax.broadcasted_iota(jnp.int32, (WK_C, WV_C), 1) // DV_C
    diag = kh == vh
    expand = jnp.where(diag, 1.0, 0.0).astype(BF16)
    pad_row = lax.broadcasted_iota(jnp.int32, (LANE, WK_C), 0)
    pad_col = lax.broadcasted_iota(jnp.int32, (WK_C, LANE), 1)

    def chunk(ci, carry):
        r0 = pl.multiple_of(ci * c, c)
        q = q_ref[pl.ds(r0, c), :] * (DK_C ** -0.5)
        k = k_ref[pl.ds(r0, c), :]
        v = v_ref[pl.ds(r0, c), :]
        x = _dot(rc_ref[pl.ds(r0, c), :].astype(BF16), w2_ref[...]) + b_ref[...]
        la = (jnp.minimum(x, 0.0) - jnp.log(1.0 + jnp.exp(-jnp.abs(x)))) * (1.0 / GLA_GATE_NORM)
        bc = _dot(tril, la, precision=HIGHEST)
        ps = []
        for t in range(c):
            diff = bc[t:t + 1, :] - bc
            dec = jnp.exp(jnp.where(s_idx <= t, diff, -jnp.inf))
            ps.append((q[t:t + 1, :] * dec) * k)
        a = _dot(jnp.concatenate(ps, axis=0).astype(BF16), expand)
        o_rows = [jnp.sum(a[t * c:(t + 1) * c, :] * v, axis=0, keepdims=True) for t in range(c)]
        st = st_ref[...]
        o = jnp.concatenate(o_rows, axis=0) + _dot((q * jnp.exp(bc)).astype(BF16), st.astype(BF16))
        o_ref[pl.ds(r0, c), :] = o
        b_last = bc[c - 1:c, :]
        kd = k * jnp.exp(b_last - bc)
        packed = jnp.where(pad_row == c, jnp.broadcast_to(b_last, (LANE, WK_C)), _pad_rows(kd, LANE))
        packed_t = packed.T
        g_col = jnp.exp(packed_t[:, c:c + 1])
        kd_t = jnp.where(pad_col < c, packed_t, 0.0).astype(BF16)
        ds = _dot(kd_t, _pad_rows(v, LANE).astype(BF16))
        st_ref[...] = st * g_col + jnp.where(diag, ds, 0.0)
        return carry

    lax.fori_loop(0, nchunk, chunk, 0)

    @pl.when(step == pl.num_programs(1) - 1)
    def _():
        for h in range(H_C):
            sfin_ref[0, h] = st_ref[h * DK_C:(h + 1) * DK_C, h * DV_C:(h + 1) * DV_C]


def gla(z, w2, b, s0, n, length, *, tb=256):
    c = math.gcd(length, GLA_CHUNK)
    tb = math.gcd(length, tb)
    steps = length // tb
    w2p = jnp.concatenate([w2, jnp.zeros((LANE - GLA_RANK, WK_C), w2.dtype)], axis=0).astype(BF16)
    row = lambda i, s: i * steps + s
    return pl.pallas_call(
        functools.partial(_gla_kernel, c=c, nchunk=tb // c),
        grid=(n, steps),
        in_specs=[
            pl.BlockSpec((tb, WK_C), lambda i, s: (row(i, s), CB_QC * LANE // WK_C)),
            pl.BlockSpec((tb, WK_C), lambda i, s: (row(i, s), CB_KC * LANE // WK_C)),
            pl.BlockSpec((tb, WV_C), lambda i, s: (row(i, s), CB_VC * LANE // WV_C)),
            pl.BlockSpec((tb, LANE), lambda i, s: (row(i, s), CB_RC)),
            pl.BlockSpec((LANE, WK_C), lambda i, s: (0, 0)),
            pl.BlockSpec((1, WK_C), lambda i, s: (0, 0)),
            pl.BlockSpec((1, H_C, DK_C, DV_C), lambda i, s: (i, 0, 0, 0)),
        ],
        out_specs=[
            pl.BlockSpec((tb, WV_C), lambda i, s: (row(i, s), 0)),
            pl.BlockSpec((1, H_C, DK_C, DV_C), lambda i, s: (i, 0, 0, 0)),
        ],
        out_shape=[jax.ShapeDtypeStruct((n * length, WV_C), F32), jax.ShapeDtypeStruct((n, H_C, DK_C, DV_C), F32)],
        scratch_shapes=[pltpu.VMEM((WK_C, WV_C), F32)],
        compiler_params=_params("parallel", "arbitrary"),
        name="gla",
    )(z, z, z, z, w2p, b.reshape(1, WK_C), s0)


def _branch_kernel(ma_ref, mb_ref, mc_ref, ga_ref, gb_ref, gc_ref, oa_ref, ob_ref, oc_ref, gain_ref,
                   wa_ref, wb_ref, wc_ref, m_ref):
    silu = lambda g: g * _sigmoid(g)
    ya = _dot((oa_ref[...] * silu(ga_ref[...])).astype(BF16), wa_ref[...])
    yb = _dot((ob_ref[...] * silu(gb_ref[...])).astype(BF16), wb_ref[...])
    oc = oc_ref[...]
    gain = gain_ref[...]
    heads = []
    for h in range(H_C):
        oh = oc[:, h * DV_C:(h + 1) * DV_C]
        heads.append(oh * lax.rsqrt(jnp.mean(oh * oh, axis=-1, keepdims=True) + RMS_EPS) * gain)
    ocn = jnp.concatenate(heads, axis=-1)
    yc = _dot((ocn * silu(gc_ref[...])).astype(BF16), wc_ref[...])
    m = _sigmoid(ma_ref[...]) * ya + _sigmoid(mb_ref[...]) * yb + _sigmoid(mc_ref[...]) * yc
    m_ref[...] = m.astype(BF16)


def branch_merge(z, oa, ob, oc, gla_gain, wa, wb, wc, layer, *, tm=256):
    m = z.shape[0]
    tm = math.gcd(m, tm)
    zb = lambda cb, w: pl.BlockSpec((tm, w), lambda i: (i, cb * LANE // w))
    rows = lambda w: pl.BlockSpec((tm, w), lambda i: (i, 0))
    wspec = lambda w: pl.BlockSpec((None, w, D_MODEL), lambda i: (layer, 0, 0))
    return pl.pallas_call(
        _branch_kernel,
        grid=(m // tm,),
        in_specs=[zb(CB_MA, D_MODEL), zb(CB_MB, D_MODEL), zb(CB_MC, D_MODEL),
                  zb(CB_GA, W_A), zb(CB_GB, W_B), zb(CB_GC, WV_C),
                  rows(W_A), rows(W_B), rows(WV_C), pl.BlockSpec((1, DV_C), lambda i: (0, 0)),
                  wspec(W_A), wspec(W_B), wspec(WV_C)],
        out_specs=rows(D_MODEL),
        out_shape=jax.ShapeDtypeStruct((m, D_MODEL), BF16),
        compiler_params=_params("parallel"),
        name="branch_merge",
    )(z, z, z, z, z, z, oa, ob, oc, gla_gain.reshape(1, DV_C), wa, wb, wc)


def _outproj_kernel(x_ref, m_ref, w_ref, g_ref, o_ref, *, final):
    y = x_ref[...] + _dot(m_ref[...], w_ref[...])
    if final:
        y = y * lax.rsqrt(jnp.mean(y * y, axis=-1, keepdims=True) + RMS_EPS) * g_ref[...]
    o_ref[...] = y


def out_project(x, m, w_out, final_gain, layer, *, final, tm=512):
    rows, d = x.shape
    tm = math.gcd(rows, tm)
    return pl.pallas_call(
        functools.partial(_outproj_kernel, final=final),
        grid=(rows // tm,),
        in_specs=[pl.BlockSpec((tm, d), lambda i: (i, 0)), pl.BlockSpec((tm, d), lambda i: (i, 0)),
                  pl.BlockSpec((None, d, d), lambda i: (layer, 0, 0)), pl.BlockSpec((1, d), lambda i: (0, 0))],
        out_specs=pl.BlockSpec((tm, d), lambda i: (i, 0)),
        out_shape=jax.ShapeDtypeStruct((rows, d), F32),
        compiler_params=_params("parallel"),
        name="out_project",
    )(x, m, w_out, final_gain.reshape(1, d))


def _kv_writer_kernel(*refs, heads, n_layers):
    nseg = len(heads)
    ins, outs = refs[:n_layers * nseg], refs[n_layers * nseg:]
    layer = pl.program_id(0)
    for li in range(n_layers):
        @pl.when(layer == li)
        def _(li=li):
            for sg, nh in enumerate(heads):
                src, dst = ins[li * nseg + sg], outs[sg]
                for h in range(nh):
                    dst[0, pl.ds(h, src.shape[0], stride=nh), :] = src[:, h * D_HEAD:(h + 1) * D_HEAD]


def kv_rows(zs, *, tm=512):
    n_layers = len(zs)
    rows = zs[0].shape[0]
    tm = math.gcd(rows, tm)
    nt = rows // tm
    segs = ((CB_KA, H_A), (CB_VA, H_A), (CB_KB, H_B), (CB_VB, H_B))
    heads = tuple(nh for _, nh in segs)

    def in_spec(li, cb, nh):
        w = nh * D_HEAD
        return pl.BlockSpec((tm, w), lambda l, i: (jnp.clip(i + (l - li) * nt, 0, nt - 1), cb * LANE // w))

    return pl.pallas_call(
        functools.partial(_kv_writer_kernel, heads=heads, n_layers=n_layers),
        grid=(n_layers, nt),
        in_specs=[in_spec(li, cb, nh) for li in range(n_layers) for cb, nh in segs],
        out_specs=[pl.BlockSpec((1, tm * nh, D_HEAD), lambda l, i: (l, i, 0)) for nh in heads],
        out_shape=[jax.ShapeDtypeStruct((n_layers, rows * nh, D_HEAD), F32) for nh in heads],
        compiler_params=_params("arbitrary", "arbitrary"),
        name="kv_rows",
    )(*[z for z in zs for _ in segs])


def kernel(x_prompt, x_sample, cache_moba_k, cache_moba_v, cache_sb_k, cache_sb_v, state_gla, page_table,
           norm_gain, w_in, gla_w2, gla_b, gla_out_gain, w_br_a, w_br_b, w_br_c, w_out, final_gain):
    n_seq, seq_len, _ = x_prompt.shape
    n_dec, dec_len, _ = x_sample.shape
    n_pool = cache_moba_k.shape[1]
    x_p = x_prompt.reshape(n_seq * seq_len, D_MODEL)
    x_s = x_sample.reshape(n_dec * dec_len, D_MODEL)
    w_all = jnp.concatenate(
        [w_in[:, :, MAIN_W + GLA_RANK:], w_in[:, :, :MAIN_W], w_in[:, :, MAIN_W:MAIN_W + GLA_RANK],
         jnp.zeros((DEPTH, D_MODEL, RC_PAD - GLA_RANK), w_in.dtype)], axis=-1).astype(BF16)
    wa, wb, wc, wo = (w.astype(BF16) for w in (w_br_a, w_br_b, w_br_c, w_out))
    pool_ak = cache_moba_k.reshape(DEPTH, n_pool, PAGE_SIZE * H_A, D_HEAD)
    pool_av = cache_moba_v.reshape(DEPTH, n_pool, PAGE_SIZE * H_A, D_HEAD)
    pool_bk = cache_sb_k.reshape(DEPTH, n_pool, PAGE_SIZE * H_B, D_HEAD)
    pool_bv = cache_sb_v.reshape(DEPTH, n_pool, PAGE_SIZE * H_B, D_HEAD)
    zero_state = jnp.zeros((n_seq, H_C, DK_C, DV_C), F32)

    zs_p, zs_s, st_p, st_s = [], [], [], []
    for l in range(DEPTH):
        last = l == DEPTH - 1
        z = norm_project(x_p, norm_gain[l], w_all, l)
        oa = moba_prompt_attn(z, n_seq, seq_len)
        ob = sb_prompt_attn(z, n_seq, seq_len)
        oc, sf = gla(z, gla_w2[l], gla_b[l], zero_state, n_seq, seq_len)
        m = branch_merge(z, oa, ob, oc, gla_out_gain[l], wa, wb, wc, l)
        x_p = out_project(x_p, m, wo, final_gain, l, final=last)
        zs_p.append(z)
        st_p.append(sf)
        z = norm_project(x_s, norm_gain[l], w_all, l)
        oa = moba_sample_attn(z, pool_ak, pool_av, page_table, l, n_dec, dec_len)
        ob = sb_sample_attn(z, pool_bk, pool_bv, page_table, l, n_dec, dec_len)
        oc, sf = gla(z, gla_w2[l], gla_b[l], state_gla[l].astype(F32), n_dec, dec_len)
        m = branch_merge(z, oa, ob, oc, gla_out_gain[l], wa, wb, wc, l)
        x_s = out_project(x_s, m, wo, final_gain, l, final=last)
        zs_s.append(z)
        st_s.append(sf)

    kv_p = kv_rows(zs_p)
    kv_s = kv_rows(zs_s)
    shape_p = lambda a, nh: a.reshape(DEPTH, n_seq, seq_len, nh, D_HEAD)
    shape_s = lambda a, nh: a.reshape(DEPTH, n_dec, dec_len, nh, D_HEAD)
    return (x_p.reshape(n_seq, seq_len, D_MODEL), x_s.reshape(n_dec, dec_len, D_MODEL),
            shape_p(kv_p[0], H_A), shape_p(kv_p[1], H_A), shape_p(kv_p[2], H_B), shape_p(kv_p[3], H_B),
            jnp.stack(st_p),
            shape_s(kv_s[0], H_A), shape_s(kv_s[1], H_A), shape_s(kv_s[2], H_B), shape_s(kv_s[3], H_B),
            jnp.stack(st_s))
```

```python
import functools
import math

import jax
import jax.numpy as jnp
from jax import lax
from jax.experimental import pallas as pl
from jax.experimental.pallas import tpu as pltpu

D_MODEL = 2048
DEPTH = 2
PAGE_SIZE = 128
D_HEAD = 128
H_A = 8
H_B = 4
H_C = 4
DK_C = 64
DV_C = 128
W_A = H_A * D_HEAD
W_B = H_B * D_HEAD
WK_C = H_C * DK_C
WV_C = H_C * DV_C
GLA_RANK = 16
GLA_GATE_NORM = 16.0
GLA_CHUNK = 16
MOBA_BLOCK = 256
MOBA_TOPK = 3
RMS_EPS = 1e-6

LANE = 128
MAIN_W = 4 * W_A + 4 * W_B + 2 * WK_C + 2 * WV_C
GATE_W = 3 * D_MODEL
RC_PAD = 512
Z_W = GATE_W + MAIN_W + RC_PAD
CB_MA, CB_MB, CB_MC = 0, 16, 32
CB_QA, CB_KA, CB_VA, CB_GA = 48, 56, 64, 72
CB_QB, CB_KB, CB_VB, CB_GB = 80, 84, 88, 92
CB_QC, CB_KC, CB_VC, CB_GC = 96, 98, 100, 104
CB_RC = 108

NEG = -1e30
SB_LOG_FLOOR = -104.0
VMEM_LIMIT = 48 * 1024 * 1024
MOBA_HEADS_PER_STEP = 2

F32 = jnp.float32
BF16 = jnp.bfloat16
HIGHEST = lax.Precision.HIGHEST


def _dot_t(a, b, **kw):
    return lax.dot_general(a, b, (((1,), (1,)), ((), ())), preferred_element_type=F32, **kw)


def _dot(a, b, **kw):
    return jnp.dot(a, b, preferred_element_type=F32, **kw)


def _softplus(z):
    return jnp.maximum(z, 0.0) + jnp.log(1.0 + jnp.exp(-jnp.abs(z)))


def _sigmoid(z):
    return 1.0 / (1.0 + jnp.exp(-z))


def _split_bf16(x):
    hi = x.astype(BF16)
    lo = (x - hi.astype(F32)).astype(BF16)
    return hi, lo


def _params(*sem):
    return pltpu.CompilerParams(dimension_semantics=sem, vmem_limit_bytes=VMEM_LIMIT)


def _pad_rows(x, rows):
    return jnp.concatenate([x, jnp.zeros((rows - x.shape[0], x.shape[1]), x.dtype)], axis=0)


def _head_cols(x, h):
    return x[:, h * D_HEAD:(h + 1) * D_HEAD]


def _proj_kernel(x_ref, g_ref, w_ref, o_ref, xn_ref):
    @pl.when(pl.program_id(1) == 0)
    def _():
        x = x_ref[...]
        ms = jnp.mean(x * x, axis=-1, keepdims=True)
        xn_ref[...] = (x * lax.rsqrt(ms + RMS_EPS) * g_ref[...]).astype(BF16)

    o_ref[...] = _dot(xn_ref[...], w_ref[...])


def norm_project(x, gain, w_bf16, layer, *, tm=1024, tn=1024):
    m, d = x.shape
    n = w_bf16.shape[2]
    tm = math.gcd(m, tm)
    return pl.pallas_call(
        _proj_kernel,
        grid=(m // tm, n // tn),
        in_specs=[
            pl.BlockSpec((tm, d), lambda i, j: (i, 0)),
            pl.BlockSpec((1, d), lambda i, j: (0, 0)),
            pl.BlockSpec((None, d, tn), lambda i, j: (layer, 0, j)),
        ],
        out_specs=pl.BlockSpec((tm, tn), lambda i, j: (i, j)),
        out_shape=jax.ShapeDtypeStruct((m, n), F32),
        scratch_shapes=[pltpu.VMEM((tm, d), BF16)],
        compiler_params=_params("parallel", "arbitrary"),
        name="norm_project",
    )(x, gain.reshape(1, d), w_bf16)


def _topk_select_t(sc_t, allowed, nblk):
    blk_i = lax.broadcasted_iota(jnp.int32, sc_t.shape, 0)
    sc_t = jnp.where(allowed, sc_t, -jnp.inf)
    rank = jnp.zeros(sc_t.shape, jnp.int32)
    for i in range(nblk):
        si = sc_t[i:i + 1, :]
        ge = (si >= sc_t).astype(jnp.int32)
        gt = (si > sc_t).astype(jnp.int32)
        rank = rank + jnp.where(blk_i > i, ge, gt)
    return jnp.where(allowed, (rank < MOBA_TOPK).astype(F32), 0.0)


def _moba_prompt_kernel(q_ref, k_ref, v_ref, spread_ref, o_ref, means_ref, bias_ref, *, nblk, hp):
    qi = pl.program_id(2)
    blk = MOBA_BLOCK
    scale = D_HEAD ** -0.5
    hcols = lambda h: slice(h * D_HEAD, (h + 1) * D_HEAD)

    @pl.when(qi == 0)
    def _():
        for h in range(hp):
            for j in range(nblk):
                means_ref[h, j:j + 1, :] = (
                    jnp.sum(k_ref[j * blk:(j + 1) * blk, hcols(h)], axis=0, keepdims=True) * (1.0 / blk))

    d0 = pl.multiple_of(qi * blk, blk)
    r_i = lax.broadcasted_iota(jnp.int32, (blk, blk), 0)
    c_i = lax.broadcasted_iota(jnp.int32, (blk, blk), 1)
    qss, init = [], []
    for h in range(hp):
        q = q_ref[:, hcols(h)]
        sc_t = _dot_t(means_ref[h], q, precision=HIGHEST)
        blk_i = lax.broadcasted_iota(jnp.int32, sc_t.shape, 0)
        sel_t = _topk_select_t(sc_t, blk_i < qi, nblk)
        sel = _pad_rows(sel_t, LANE).T.astype(BF16)
        flags = _dot(sel, spread_ref[...])
        for j in range(nblk):
            bias_ref[h, j] = (flags[:, j * LANE:(j + 1) * LANE] - 1.0) * (-NEG)
        qs = (q * scale).astype(BF16)
        s = _dot_t(qs, k_ref[pl.ds(d0, blk), hcols(h)].astype(BF16))
        s = jnp.where(c_i <= r_i, s, NEG)
        m = jnp.max(s, axis=-1, keepdims=True)
        p = jnp.exp(s - m)
        l = jnp.sum(p, axis=-1, keepdims=True)
        acc = _dot(p.astype(BF16), v_ref[pl.ds(d0, blk), hcols(h)].astype(BF16))
        qss.append(qs)
        init.append((m, l, acc))

    def body(t, carry):
        j = 2 * t
        j0 = pl.multiple_of(j * blk, 2 * blk)
        out = []
        for h in range(hp):
            m, l, acc = carry[h]
            kj = k_ref[pl.ds(j0, 2 * blk), hcols(h)].astype(BF16)
            vj = v_ref[pl.ds(j0, 2 * blk), hcols(h)].astype(BF16)
            bias = jnp.concatenate(
                [bias_ref[h, j]] * (blk // LANE) + [bias_ref[h, j + 1]] * (blk // LANE), axis=-1)
            s = _dot_t(qss[h], kj) + bias
            m_new = jnp.maximum(m, jnp.max(s, axis=-1, keepdims=True))
            alpha = jnp.exp(m - m_new)
            p = jnp.exp(s - m_new)
            l = alpha * l + jnp.sum(p, axis=-1, keepdims=True)
            acc = alpha * acc + _dot(p.astype(BF16), vj)
            out.append((m_new, l, acc))
        return tuple(out)

    fin = lax.fori_loop(0, (qi + 1) // 2, body, tuple(init))
    o_ref[...] = jnp.concatenate([acc / l for (_, l, acc) in fin], axis=-1)


def moba_prompt_attn(z, n_seq, seq_len):
    blk = MOBA_BLOCK
    nblk = seq_len // blk
    hp = MOBA_HEADS_PER_STEP
    assert nblk % 2 == 0 and H_A % hp == 0
    w = hp * D_HEAD
    spread = (jnp.arange(LANE)[:, None] == jnp.arange(nblk * LANE)[None, :] // LANE).astype(BF16)
    return pl.pallas_call(
        functools.partial(_moba_prompt_kernel, nblk=nblk, hp=hp),
        grid=(n_seq, H_A // hp, nblk),
        in_specs=[
            pl.BlockSpec((blk, w), lambda n, h, i: (n * nblk + i, CB_QA // hp + h)),
            pl.BlockSpec((seq_len, w), lambda n, h, i: (n, CB_KA // hp + h)),
            pl.BlockSpec((seq_len, w), lambda n, h, i: (n, CB_VA // hp + h)),
            pl.BlockSpec((LANE, nblk * LANE), lambda n, h, i: (0, 0)),
        ],
        out_specs=pl.BlockSpec((blk, w), lambda n, h, i: (n * nblk + i, h)),
        out_shape=jax.ShapeDtypeStruct((n_seq * seq_len, W_A), F32),
        scratch_shapes=[pltpu.VMEM((hp, nblk, D_HEAD), F32), pltpu.VMEM((hp, nblk, blk, LANE), F32)],
        compiler_params=_params("parallel", "parallel", "arbitrary"),
        name="moba_prompt",
    )(z, z, z, spread)


def _sb_block(zl, c, upper, mask):
    sp = _softplus(zl)
    lk = -sp
    if mask is not None:
        lk = jnp.where(mask, lk, 0.0)
    hi, lo = _split_bf16(lk)
    after = _dot(hi, upper) + _dot(lo, upper)
    a = jnp.exp((zl - sp) + after + c)
    if mask is not None:
        a = jnp.where(mask, a, 0.0)
    return a, c + after[:, 0:1] + lk[:, 0:1]


def _upper(n):
    r_i = lax.broadcasted_iota(jnp.int32, (n, n), 0)
    c_i = lax.broadcasted_iota(jnp.int32, (n, n), 1)
    return jnp.where(r_i > c_i, 1.0, 0.0).astype(BF16), c_i < r_i


def _sb_prompt_kernel(q_ref, k_ref, v_ref, o_ref, *, blk):
    qi = pl.program_id(2)
    scale = D_HEAD ** -0.5
    qs = (q_ref[...] * scale).astype(BF16)
    upper, causal = _upper(blk)

    def block(j, c, acc, mask):
        j0 = pl.multiple_of(j * blk, blk)
        kj = k_ref[pl.ds(j0, blk), :].astype(BF16)
        vj = v_ref[pl.ds(j0, blk), :].astype(BF16)
        a, c = _sb_block(_dot_t(qs, kj), c, upper, mask)
        return c, acc + _dot(a.astype(BF16), vj)

    c, acc = block(qi, jnp.zeros((blk, 1), F32), jnp.zeros((blk, D_HEAD), F32), causal)

    def cond(carry):
        t, c, _ = carry
        return jnp.logical_and(t < qi, jnp.max(c) > SB_LOG_FLOOR)

    def body(carry):
        t, c, acc = carry
        c, acc = block(qi - 1 - t, c, acc, None)
        return t + 1, c, acc

    _, c, acc = lax.while_loop(cond, body, (jnp.int32(0), c, acc))
    o_ref[...] = acc


def sb_prompt_attn(z, n_seq, seq_len, *, blk=256):
    nblk = seq_len // blk
    return pl.pallas_call(
        functools.partial(_sb_prompt_kernel, blk=blk),
        grid=(n_seq, H_B, nblk),
        in_specs=[
            pl.BlockSpec((blk, D_HEAD), lambda n, h, i: (n * nblk + i, CB_QB + h)),
            pl.BlockSpec((seq_len, D_HEAD), lambda n, h, i: (n, CB_KB + h)),
            pl.BlockSpec((seq_len, D_HEAD), lambda n, h, i: (n, CB_VB + h)),
        ],
        out_specs=pl.BlockSpec((blk, D_HEAD), lambda n, h, i: (n * nblk + i, h)),
        out_shape=jax.ShapeDtypeStruct((n_seq * seq_len, W_B), F32),
        compiler_params=_params("parallel", "parallel", "arbitrary"),
        name="sb_prompt",
    )(z, z, z)


def _page_head(ref, h, n_heads):
    return ref[0, 0, pl.ds(h, PAGE_SIZE, stride=n_heads), :]


def _stack_heads(x, n_heads):
    return jnp.concatenate([_head_cols(x, h) for h in range(n_heads)], axis=0)


def _unstack_heads(x, n_heads, t):
    return jnp.concatenate([x[h * t:(h + 1) * t, :] for h in range(n_heads)], axis=-1)


def _moba_sample_kernel(pt_ref, q_ref, kn_ref, vn_ref, *refs, t, n_pages, pg):
    k_refs = refs[:pg]
    v_refs = refs[pg:2 * pg]
    o_ref = refs[2 * pg]
    s_ref, p_ref, ksum_ref, acc_ref, l_ref = refs[2 * pg + 1:]
    del pt_ref
    s = pl.program_id(1)
    ng = n_pages // pg
    ppb = MOBA_BLOCK // PAGE_SIZE
    nblk = n_pages // ppb
    scale = D_HEAD ** -0.5
    rows = H_A * t
    qf = _stack_heads(q_ref[...], H_A)
    qb = (qf * scale).astype(BF16)
    r_i = lax.broadcasted_iota(jnp.int32, (rows, PAGE_SIZE * H_A), 0)
    c_i = lax.broadcasted_iota(jnp.int32, (rows, PAGE_SIZE * H_A), 1)
    head_bias = jnp.where(c_i % H_A == r_i // t, 0.0, NEG)

    @pl.when(s < ng)
    def _():
        for i in range(pg):
            kp = k_refs[i][0, 0]
            ksum_ref[s * pg + i] = jnp.sum(kp.reshape(PAGE_SIZE, H_A, D_HEAD), axis=0)
            s_ref[s * pg + i] = _dot_t(qb, kp.astype(BF16)) + head_bias

    @pl.when(s == ng - 1)
    def _():
        sc = []
        for j in range(nblk):
            mean_j = sum(ksum_ref[j * ppb + i] for i in range(ppb)) * (1.0 / MOBA_BLOCK)
            mexp = jnp.concatenate([jnp.broadcast_to(mean_j[h:h + 1, :], (t, D_HEAD)) for h in range(H_A)], axis=0)
            sc.append(jnp.sum(qf * mexp, axis=-1, keepdims=True))
        bias = []
        for j in range(nblk):
            rank = sum(((sc[i] >= sc[j]) if i < j else (sc[i] > sc[j])).astype(jnp.int32)
                       for i in range(nblk) if i != j)
            bias.append(jnp.where(rank < MOBA_TOPK, 0.0, NEG))
        lo = _dot_t(qb, _pad_rows(_stack_heads(kn_ref[...], H_A), LANE).astype(BF16))
        r_o = lax.broadcasted_iota(jnp.int32, lo.shape, 0)
        c_o = lax.broadcasted_iota(jnp.int32, lo.shape, 1)
        lo = jnp.where((c_o // t == r_o // t) & (c_o % t <= r_o % t), lo, NEG)
        m = jnp.max(lo, axis=-1, keepdims=True)
        for pi in range(n_pages):
            m = jnp.maximum(m, jnp.max(s_ref[pi] + bias[pi // ppb], axis=-1, keepdims=True))
        po = jnp.exp(lo - m)
        l = jnp.sum(po, axis=-1, keepdims=True)
        for pi in range(n_pages):
            pp = jnp.exp(s_ref[pi] + bias[pi // ppb] - m)
            l = l + jnp.sum(pp, axis=-1, keepdims=True)
            p_ref[pi] = pp.astype(BF16)
        l_ref[...] = jnp.broadcast_to(l, l_ref.shape)
        acc_ref[...] = _dot(po.astype(BF16), _pad_rows(_stack_heads(vn_ref[...], H_A), LANE).astype(BF16))

    @pl.when(s >= ng)
    def _():
        for i in range(pg):
            acc_ref[...] += _dot(p_ref[(s - ng) * pg + i], v_refs[i][0, 0].astype(BF16))

    @pl.when(s == 2 * ng - 1)
    def _():
        o_ref[...] = _unstack_heads(acc_ref[...] / l_ref[:, 0:1], H_A, t)


def moba_sample_attn(z, pool_k, pool_v, page_table, layer, n_dec, t, *, pg=8):
    n_pages = page_table.shape[1]
    ng = n_pages // pg
    rows = H_A * t
    zspec = lambda cb: pl.BlockSpec((t, W_A), lambda b, s, pt: (b, cb * LANE // W_A))
    page_block = (1, 1, PAGE_SIZE * H_A, D_HEAD)
    kspec = lambda i: pl.BlockSpec(
        page_block, lambda b, s, pt: (layer, pt[b, jnp.minimum(s, ng - 1) * pg + i], 0, 0))
    vspec = lambda i: pl.BlockSpec(
        page_block, lambda b, s, pt: (layer, pt[b, jnp.maximum(s - ng, 0) * pg + i], 0, 0))
    grid_spec = pltpu.PrefetchScalarGridSpec(
        num_scalar_prefetch=1,
        grid=(n_dec, 2 * ng),
        in_specs=[zspec(CB_QA), zspec(CB_KA), zspec(CB_VA)] + [kspec(i) for i in range(pg)] + [vspec(i) for i in range(pg)],
        out_specs=pl.BlockSpec((t, W_A), lambda b, s, pt: (b, 0)),
        scratch_shapes=[
            pltpu.VMEM((n_pages, rows, PAGE_SIZE * H_A), F32), pltpu.VMEM((n_pages, rows, PAGE_SIZE * H_A), BF16),
            pltpu.VMEM((n_pages, H_A, D_HEAD), F32),
            pltpu.VMEM((rows, D_HEAD), F32), pltpu.VMEM((rows, LANE), F32)],
    )
    return pl.pallas_call(
        functools.partial(_moba_sample_kernel, t=t, n_pages=n_pages, pg=pg),
        grid_spec=grid_spec,
        out_shape=jax.ShapeDtypeStruct((n_dec * t, W_A), F32),
        compiler_params=_params("parallel", "arbitrary"),
        name="moba_sample",
    )(page_table, z, z, z, *([pool_k] * pg), *([pool_v] * pg))


def _sb_sample_kernel(pt_ref, q_ref, kn_ref, vn_ref, *refs, t, pg):
    k_refs = refs[:pg]
    v_refs = refs[pg:2 * pg]
    o_ref = refs[2 * pg]
    acc_ref, c_ref, live_ref = refs[2 * pg + 1:]
    del pt_ref
    s = pl.program_id(1)
    scale = D_HEAD ** -0.5
    upper, _ = _upper(PAGE_SIZE)
    hrows = lambda h: slice(h * t, (h + 1) * t)
    qs = _stack_heads(q_ref[...], H_B) * scale

    def visit(k_of, v_of, mask):
        zl = jnp.concatenate([_dot_t(qs[hrows(h)].astype(BF16), k_of(h)) for h in range(H_B)], axis=0)
        a, c = _sb_block(zl, c_ref[:, 0:1], upper, mask)
        acc_ref[...] += jnp.concatenate([_dot(a[hrows(h)].astype(BF16), v_of(h)) for h in range(H_B)], axis=0)
        c_ref[...] = jnp.broadcast_to(c, c_ref.shape)
        live_ref[0] = (jnp.max(c) > SB_LOG_FLOOR).astype(jnp.int32)

    @pl.when(s == 0)
    def _():
        acc_ref[...] = jnp.zeros_like(acc_ref)
        c_ref[...] = jnp.zeros_like(c_ref)
        kn, vn = kn_ref[...], vn_ref[...]
        r_t = lax.broadcasted_iota(jnp.int32, (H_B * t, PAGE_SIZE), 0) % t
        c_j = lax.broadcasted_iota(jnp.int32, (H_B * t, PAGE_SIZE), 1)
        visit(lambda h: _pad_rows(_head_cols(kn, h), PAGE_SIZE).astype(BF16),
              lambda h: _pad_rows(_head_cols(vn, h), PAGE_SIZE).astype(BF16), c_j < r_t)

    for i in range(pg):
        @pl.when(live_ref[0] > 0)
        def _(i=i):
            visit(lambda h: _page_head(k_refs[i], h, H_B).astype(BF16),
                  lambda h: _page_head(v_refs[i], h, H_B).astype(BF16), None)

    @pl.when(s == pl.num_programs(1) - 1)
    def _():
        o_ref[...] = _unstack_heads(acc_ref[...], H_B, t)


def sb_sample_attn(z, pool_k, pool_v, page_table, layer, n_dec, t, *, pg=8):
    n_pages = page_table.shape[1]
    ng = n_pages // pg
    rows = H_B * t
    zspec = lambda cb: pl.BlockSpec((t, W_B), lambda b, s, pt: (b, cb * LANE // W_B))
    pspec = lambda i: pl.BlockSpec(
        (1, 1, PAGE_SIZE * H_B, D_HEAD), lambda b, s, pt: (layer, pt[b, n_pages - 1 - (s * pg + i)], 0, 0))
    grid_spec = pltpu.PrefetchScalarGridSpec(
        num_scalar_prefetch=1,
        grid=(n_dec, ng),
        in_specs=[zspec(CB_QB), zspec(CB_KB), zspec(CB_VB)] + [pspec(i) for i in range(pg)] * 2,
        out_specs=pl.BlockSpec((t, W_B), lambda b, s, pt: (b, 0)),
        scratch_shapes=[pltpu.VMEM((rows, D_HEAD), F32), pltpu.VMEM((rows, LANE), F32), pltpu.SMEM((1,), jnp.int32)],
    )
    return pl.pallas_call(
        functools.partial(_sb_sample_kernel, t=t, pg=pg),
        grid_spec=grid_spec,
        out_shape=jax.ShapeDtypeStruct((n_dec * t, W_B), F32),
        compiler_params=_params("parallel", "arbitrary"),
        name="sb_sample",
    )(page_table, z, z, z, *([pool_k] * pg), *([pool_v] * pg))


def _gla_kernel(q_ref, k_ref, v_ref, rc_ref, w2_ref, b_ref, s0_ref, o_ref, sfin_ref, st_ref, *, c, nchunk):
    step = pl.program_id(1)

    @pl.when(step == 0)
    def _():
        st_ref[...] = jnp.zeros_like(st_ref)
        for h in range(H_C):
            st_ref[h * DK_C:(h + 1) * DK_C, h * DV_C:(h + 1) * DV_C] = s0_ref[0, h]

    r_i = lax.broadcasted_iota(jnp.int32, (c, c), 0)
    c_i = lax.broadcasted_iota(jnp.int32, (c, c), 1)
    tril = jnp.where(c_i <= r_i, 1.0, 0.0).astype(F32)
    s_idx = lax.broadcasted_iota(jnp.int32, (c, WK_C), 0)
    kh = lax.broadcasted_iota(jnp.int32, (WK_C, WV_C), 0) // DK_C
    vh = lax.broadcasted_iota(jnp.int32, (WK_C, WV_C), 1) // DV_C
    diag = kh == vh
    expand = jnp.where(diag, 1.0, 0.0).astype(BF16)
    pad_row = lax.broadcasted_iota(jnp.int32, (LANE, WK_C), 0)
    pad_col = lax.broadcasted_iota(jnp.int32, (WK_C, LANE), 1)

    def chunk(ci, carry):
        r0 = pl.multiple_of(ci * c, c)
        q = q_ref[pl.ds(r0, c), :] * (DK_C ** -0.5)
        k = k_ref[pl.ds(r0, c), :]
        v = v_ref[pl.ds(r0, c), :]
        x = _dot(rc_ref[pl.ds(r0, c), :].astype(BF16), w2_ref[...]) + b_ref[...]
        la = (jnp.minimum(x, 0.0) - jnp.log(1.0 + jnp.exp(-jnp.abs(x)))) * (1.0 / GLA_GATE_NORM)
        bc = _dot(tril, la, precision=HIGHEST)
        ps = []
        for t in range(c):
            diff = bc[t:t + 1, :] - bc
            dec = jnp.exp(jnp.where(s_idx <= t, diff, -jnp.inf))
            ps.append((q[t:t + 1, :] * dec) * k)
        a = _dot(jnp.concatenate(ps, axis=0).astype(BF16), expand)
        o_rows = [jnp.sum(a[t * c:(t + 1) * c, :] * v, axis=0, keepdims=True) for t in range(c)]
        st = st_ref[...]
        o = jnp.concatenate(o_rows, axis=0) + _dot((q * jnp.exp(bc)).astype(BF16), st.astype(BF16))
        o_ref[pl.ds(r0, c), :] = o
        b_last = bc[c - 1:c, :]
        kd = k * jnp.exp(b_last - bc)
        packed = jnp.where(pad_row == c, jnp.broadcast_to(b_last, (LANE, WK_C)), _pad_rows(kd, LANE))
        packed_t = packed.T
        g_col = jnp.exp(packed_t[:, c:c + 1])
        kd_t = jnp.where(pad_col < c, packed_t, 0.0).astype(BF16)
        ds = _dot(kd_t, _pad_rows(v, LANE).astype(BF16))
        st_ref[...] = st * g_col + jnp.where(diag, ds, 0.0)
        return carry

    lax.fori_loop(0, nchunk, chunk, 0)

    @pl.when(step == pl.num_programs(1) - 1)
    def _():
        for h in range(H_C):
            sfin_ref[0, h] = st_ref[h * DK_C:(h + 1) * DK_C, h * DV_C:(h + 1) * DV_C]


def gla(z, w2, b, s0, n, length, *, tb=256):
    c = math.gcd(length, GLA_CHUNK)
    tb = math.gcd(length, tb)
    steps = length // tb
    w2p = jnp.concatenate([w2, jnp.zeros((LANE - GLA_RANK, WK_C), w2.dtype)], axis=0).astype(BF16)
    row = lambda i, s: i * steps + s
    return pl.pallas_call(
        functools.partial(_gla_kernel, c=c, nchunk=tb // c),
        grid=(n, steps),
        in_specs=[
            pl.BlockSpec((tb, WK_C), lambda i, s: (row(i, s), CB_QC * LANE // WK_C)),
            pl.BlockSpec((tb, WK_C), lambda i, s: (row(i, s), CB_KC * LANE // WK_C)),
            pl.BlockSpec((tb, WV_C), lambda i, s: (row(i, s), CB_VC * LANE // WV_C)),
            pl.BlockSpec((tb, LANE), lambda i, s: (row(i, s), CB_RC)),
            pl.BlockSpec((LANE, WK_C), lambda i, s: (0, 0)),
            pl.BlockSpec((1, WK_C), lambda i, s: (0, 0)),
            pl.BlockSpec((1, H_C, DK_C, DV_C), lambda i, s: (i, 0, 0, 0)),
        ],
        out_specs=[
            pl.BlockSpec((tb, WV_C), lambda i, s: (row(i, s), 0)),
            pl.BlockSpec((1, H_C, DK_C, DV_C), lambda i, s: (i, 0, 0, 0)),
        ],
        out_shape=[jax.ShapeDtypeStruct((n * length, WV_C), F32), jax.ShapeDtypeStruct((n, H_C, DK_C, DV_C), F32)],
        scratch_shapes=[pltpu.VMEM((WK_C, WV_C), F32)],
        compiler_params=_params("parallel", "arbitrary"),
        name="gla",
    )(z, z, z, z, w2p, b.reshape(1, WK_C), s0)


def _branch_kernel(ma_ref, mb_ref, mc_ref, ga_ref, gb_ref, gc_ref, oa_ref, ob_ref, oc_ref, gain_ref,
                   wa_ref, wb_ref, wc_ref, m_ref):
    silu = lambda g: g * _sigmoid(g)
    ya = _dot((oa_ref[...] * silu(ga_ref[...])).astype(BF16), wa_ref[...])
    yb = _dot((ob_ref[...] * silu(gb_ref[...])).astype(BF16), wb_ref[...])
    oc = oc_ref[...]
    gain = gain_ref[...]
    heads = []
    for h in range(H_C):
        oh = oc[:, h * DV_C:(h + 1) * DV_C]
        heads.append(oh * lax.rsqrt(jnp.mean(oh * oh, axis=-1, keepdims=True) + RMS_EPS) * gain)
    ocn = jnp.concatenate(heads, axis=-1)
    yc = _dot((ocn * silu(gc_ref[...])).astype(BF16), wc_ref[...])
    m = _sigmoid(ma_ref[...]) * ya + _sigmoid(mb_ref[...]) * yb + _sigmoid(mc_ref[...]) * yc
    m_ref[...] = m.astype(BF16)


def branch_merge(z, oa, ob, oc, gla_gain, wa, wb, wc, layer, *, tm=256):
    m = z.shape[0]
    tm = math.gcd(m, tm)
    zb = lambda cb, w: pl.BlockSpec((tm, w), lambda i: (i, cb * LANE // w))
    rows = lambda w: pl.BlockSpec((tm, w), lambda i: (i, 0))
    wspec = lambda w: pl.BlockSpec((None, w, D_MODEL), lambda i: (layer, 0, 0))
    return pl.pallas_call(
        _branch_kernel,
        grid=(m // tm,),
        in_specs=[zb(CB_MA, D_MODEL), zb(CB_MB, D_MODEL), zb(CB_MC, D_MODEL),
                  zb(CB_GA, W_A), zb(CB_GB, W_B), zb(CB_GC, WV_C),
                  rows(W_A), rows(W_B), rows(WV_C), pl.BlockSpec((1, DV_C), lambda i: (0, 0)),
                  wspec(W_A), wspec(W_B), wspec(WV_C)],
        out_specs=rows(D_MODEL),
        out_shape=jax.ShapeDtypeStruct((m, D_MODEL), BF16),
        compiler_params=_params("parallel"),
        name="branch_merge",
    )(z, z, z, z, z, z, oa, ob, oc, gla_gain.reshape(1, DV_C), wa, wb, wc)


def _outproj_kernel(x_ref, m_ref, w_ref, g_ref, o_ref, *, final):
    y = x_ref[...] + _dot(m_ref[...], w_ref[...])
    if final:
        y = y * lax.rsqrt(jnp.mean(y * y, axis=-1, keepdims=True) + RMS_EPS) * g_ref[...]
    o_ref[...] = y


def out_project(x, m, w_out, final_gain, layer, *, final, tm=512):
    rows, d = x.shape
    tm = math.gcd(rows, tm)
    return pl.pallas_call(
        functools.partial(_outproj_kernel, final=final),
        grid=(rows // tm,),
        in_specs=[pl.BlockSpec((tm, d), lambda i: (i, 0)), pl.BlockSpec((tm, d), lambda i: (i, 0)),
                  pl.BlockSpec((None, d, d), lambda i: (layer, 0, 0)), pl.BlockSpec((1, d), lambda i: (0, 0))],
        out_specs=pl.BlockSpec((tm, d), lambda i: (i, 0)),
        out_shape=jax.ShapeDtypeStruct((rows, d), F32),
        compiler_params=_params("parallel"),
        name="out_project",
    )(x, m, w_out, final_gain.reshape(1, d))


def _kv_writer_kernel(*refs, heads, n_layers):
    nseg = len(heads)
    ins, outs = refs[:n_layers * nseg], refs[n_layers * nseg:]
    layer = pl.program_id(0)
    for li in range(n_layers):
        @pl.when(layer == li)
        def _(li=li):
            for sg, nh in enumerate(heads):
                src, dst = ins[li * nseg + sg], outs[sg]
                for h in range(nh):
                    dst[0, pl.ds(h, src.shape[0], stride=nh), :] = src[:, h * D_HEAD:(h + 1) * D_HEAD]


def kv_rows(zs, *, tm=512):
    n_layers = len(zs)
    rows = zs[0].shape[0]
    tm = math.gcd(rows, tm)
    nt = rows // tm
    segs = ((CB_KA, H_A), (CB_VA, H_A), (CB_KB, H_B), (CB_VB, H_B))
    heads = tuple(nh for _, nh in segs)

    def in_spec(li, cb, nh):
        w = nh * D_HEAD
        return pl.BlockSpec((tm, w), lambda l, i: (jnp.clip(i + (l - li) * nt, 0, nt - 1), cb * LANE // w))

    return pl.pallas_call(
        functools.partial(_kv_writer_kernel, heads=heads, n_layers=n_layers),
        grid=(n_layers, nt),
        in_specs=[in_spec(li, cb, nh) for li in range(n_layers) for cb, nh in segs],
        out_specs=[pl.BlockSpec((1, tm * nh, D_HEAD), lambda l, i: (l, i, 0)) for nh in heads],
        out_shape=[jax.ShapeDtypeStruct((n_layers, rows * nh, D_HEAD), F32) for nh in heads],
        compiler_params=_params("arbitrary", "arbitrary"),
        name="kv_rows",
    )(*[z for z in zs for _ in segs])


def kernel(x_prompt, x_sample, cache_moba_k, cache_moba_v, cache_sb_k, cache_sb_v, state_gla, page_table,
           norm_gain, w_in, gla_w2, gla_b, gla_out_gain, w_br_a, w_br_b, w_br_c, w_out, final_gain):
    n_seq, seq_len, _ = x_prompt.shape
    n_dec, dec_len, _ = x_sample.shape
    n_pool = cache_moba_k.shape[1]
    x_p = x_prompt.reshape(n_seq * seq_len, D_MODEL)
    x_s = x_sample.reshape(n_dec * dec_len, D_MODEL)
    w_all = jnp.concatenate(
        [w_in[:, :, MAIN_W + GLA_RANK:], w_in[:, :, :MAIN_W], w_in[:, :, MAIN_W:MAIN_W + GLA_RANK],
         jnp.zeros((DEPTH, D_MODEL, RC_PAD - GLA_RANK), w_in.dtype)], axis=-1).astype(BF16)
    wa, wb, wc, wo = (w.astype(BF16) for w in (w_br_a, w_br_b, w_br_c, w_out))
    pool_ak = cache_moba_k.reshape(DEPTH, n_pool, PAGE_SIZE * H_A, D_HEAD)
    pool_av = cache_moba_v.reshape(DEPTH, n_pool, PAGE_SIZE * H_A, D_HEAD)
    pool_bk = cache_sb_k.reshape(DEPTH, n_pool, PAGE_SIZE * H_B, D_HEAD)
    pool_bv = cache_sb_v.reshape(DEPTH, n_pool, PAGE_SIZE * H_B, D_HEAD)
    zero_state = jnp.zeros((n_seq, H_C, DK_C, DV_C), F32)

    zs_p, zs_s, st_p, st_s = [], [], [], []
    for l in range(DEPTH):
        last = l == DEPTH - 1
        z = norm_project(x_p, norm_gain[l], w_all, l)
        oa = moba_prompt_attn(z, n_seq, seq_len)
        ob = sb_prompt_attn(z, n_seq, seq_len)
        oc, sf = gla(z, gla_w2[l], gla_b[l], zero_state, n_seq, seq_len)
        m = branch_merge(z, oa, ob, oc, gla_out_gain[l], wa, wb, wc, l)
        x_p = out_project(x_p, m, wo, final_gain, l, final=last)
        zs_p.append(z)
        st_p.append(sf)
        z = norm_project(x_s, norm_gain[l], w_all, l)
        oa = moba_sample_attn(z, pool_ak, pool_av, page_table, l, n_dec, dec_len)
        ob = sb_sample_attn(z, pool_bk, pool_bv, page_table, l, n_dec, dec_len)
        oc, sf = gla(z, gla_w2[l], gla_b[l], state_gla[l].astype(F32), n_dec, dec_len)
        m = branch_merge(z, oa, ob, oc, gla_out_gain[l], wa, wb, wc, l)
        x_s = out_project(x_s, m, wo, final_gain, l, final=last)
        zs_s.append(z)
        st_s.append(sf)

    kv_p = kv_rows(zs_p)
    kv_s = kv_rows(zs_s)
    shape_p = lambda a, nh: a.reshape(DEPTH, n_seq, seq_len, nh, D_HEAD)
    shape_s = lambda a, nh: a.reshape(DEPTH, n_dec, dec_len, nh, D_HEAD)
    return (x_p.reshape(n_seq, seq_len, D_MODEL), x_s.reshape(n_dec, dec_len, D_MODEL),
            shape_p(kv_p[0], H_A), shape_p(kv_p[1], H_A), shape_p(kv_p[2], H_B), shape_p(kv_p[3], H_B),
            jnp.stack(st_p),
            shape_s(kv_s[0], H_A), shape_s(kv_s[1], H_A), shape_s(kv_s[2], H_B), shape_s(kv_s[3], H_B),
            jnp.stack(st_s))
```

```python
import functools
import math

import jax
import jax.numpy as jnp
from jax import lax
from jax.experimental import pallas as pl
from jax.experimental.pallas import tpu as pltpu

D_MODEL = 2048
DEPTH = 2
PAGE_SIZE = 128
D_HEAD = 128
H_A = 8
H_B = 4
H_C = 4
DK_C = 64
DV_C = 128
W_A = H_A * D_HEAD
W_B = H_B * D_HEAD
WK_C = H_C * DK_C
WV_C = H_C * DV_C
GLA_RANK = 16
GLA_GATE_NORM = 16.0
GLA_CHUNK = 16
MOBA_BLOCK = 256
MOBA_TOPK = 3
RMS_EPS = 1e-6

LANE = 128
MAIN_W = 4 * W_A + 4 * W_B + 2 * WK_C + 2 * WV_C
GATE_W = 3 * D_MODEL
RC_PAD = 512
Z_W = GATE_W + MAIN_W + RC_PAD
CB_MA, CB_MB, CB_MC = 0, 16, 32
CB_QA, CB_KA, CB_VA, CB_GA = 48, 56, 64, 72
CB_QB, CB_KB, CB_VB, CB_GB = 80, 84, 88, 92
CB_QC, CB_KC, CB_VC, CB_GC = 96, 98, 100, 104
CB_RC = 108

NEG = -1e30
SB_LOG_FLOOR = -104.0
VMEM_LIMIT = 48 * 1024 * 1024
MOBA_HEADS_PER_STEP = 2
MOBA_BLOCKS_PER_ITER = 2
MASK_BIG = 2.0 ** 100

F32 = jnp.float32
BF16 = jnp.bfloat16
HIGHEST = lax.Precision.HIGHEST


def _dot_t(a, b, **kw):
    return lax.dot_general(a, b, (((1,), (1,)), ((), ())), preferred_element_type=F32, **kw)


def _dot(a, b, **kw):
    return jnp.dot(a, b, preferred_element_type=F32, **kw)


def _softplus(z):
    return jnp.maximum(z, 0.0) + jnp.log(1.0 + jnp.exp(-jnp.abs(z)))


def _sigmoid(z):
    return 1.0 / (1.0 + jnp.exp(-z))


def _split_bf16(x):
    hi = x.astype(BF16)
    lo = (x - hi.astype(F32)).astype(BF16)
    return hi, lo


def _params(*sem):
    return pltpu.CompilerParams(dimension_semantics=sem, vmem_limit_bytes=VMEM_LIMIT)


def _pad_rows(x, rows):
    return jnp.concatenate([x, jnp.zeros((rows - x.shape[0], x.shape[1]), x.dtype)], axis=0)


def _head_cols(x, h):
    return x[:, h * D_HEAD:(h + 1) * D_HEAD]


def _proj_kernel(x_ref, g_ref, w_ref, o_ref, xn_ref):
    @pl.when(pl.program_id(1) == 0)
    def _():
        x = x_ref[...]
        ms = jnp.mean(x * x, axis=-1, keepdims=True)
        xn_ref[...] = (x * lax.rsqrt(ms + RMS_EPS) * g_ref[...]).astype(BF16)

    o_ref[...] = _dot(xn_ref[...], w_ref[...])


def norm_project(x, gain, w_bf16, layer, *, tm=1024, tn=1024):
    m, d = x.shape
    n = w_bf16.shape[2]
    tm = math.gcd(m, tm)
    return pl.pallas_call(
        _proj_kernel,
        grid=(m // tm, n // tn),
        in_specs=[
            pl.BlockSpec((tm, d), lambda i, j: (i, 0)),
            pl.BlockSpec((1, d), lambda i, j: (0, 0)),
            pl.BlockSpec((None, d, tn), lambda i, j: (layer, 0, j)),
        ],
        out_specs=pl.BlockSpec((tm, tn), lambda i, j: (i, j)),
        out_shape=jax.ShapeDtypeStruct((m, n), F32),
        scratch_shapes=[pltpu.VMEM((tm, d), BF16)],
        compiler_params=_params("parallel", "arbitrary"),
        name="norm_project",
    )(x, gain.reshape(1, d), w_bf16)


def _topk_select_t(sc_t, allowed, nblk):
    blk_i = lax.broadcasted_iota(jnp.int32, sc_t.shape, 0)
    sc_t = jnp.where(allowed, sc_t, -jnp.inf)
    rank = jnp.zeros(sc_t.shape, jnp.int32)
    for i in range(nblk):
        si = sc_t[i:i + 1, :]
        ge = (si >= sc_t).astype(jnp.int32)
        gt = (si > sc_t).astype(jnp.int32)
        rank = rank + jnp.where(blk_i > i, ge, gt)
    return jnp.where(allowed, (rank < MOBA_TOPK).astype(F32), 0.0)


def _moba_prompt_kernel(q_ref, k_ref, v_ref, spread_ref, o_ref, means_ref, bias_ref, *, nblk, hp, kb):
    qi = pl.program_id(2)
    blk = MOBA_BLOCK
    scale = D_HEAD ** -0.5
    hcols = lambda h: slice(h * D_HEAD, (h + 1) * D_HEAD)

    @pl.when(qi == 0)
    def _():
        for h in range(hp):
            for j in range(nblk):
                means_ref[h, j:j + 1, :] = (
                    jnp.sum(k_ref[j * blk:(j + 1) * blk, hcols(h)], axis=0, keepdims=True) * (1.0 / blk))

    d0 = pl.multiple_of(qi * blk, blk)
    r_i = lax.broadcasted_iota(jnp.int32, (blk, blk), 0)
    c_i = lax.broadcasted_iota(jnp.int32, (blk, blk), 1)
    qss, init = [], []
    for h in range(hp):
        q = q_ref[:, hcols(h)]
        sc_t = _dot_t(means_ref[h], q, precision=HIGHEST)
        blk_i = lax.broadcasted_iota(jnp.int32, sc_t.shape, 0)
        sel_t = _topk_select_t(sc_t, blk_i < qi, nblk)
        pad = jnp.where(lax.broadcasted_iota(jnp.int32, (LANE - nblk, blk), 0) == LANE - nblk - 1, 1.0, 0.0)
        sel = jnp.concatenate([sel_t, pad], axis=0).T.astype(BF16)
        bias = _dot(sel, spread_ref[...])
        for j in range(nblk):
            bias_ref[h, j] = bias[:, j * LANE:(j + 1) * LANE]
        qs = (q * scale).astype(BF16)
        s = _dot_t(qs, k_ref[pl.ds(d0, blk), hcols(h)].astype(BF16))
        s = jnp.where(c_i <= r_i, s, NEG)
        m = jnp.max(s, axis=-1, keepdims=True)
        p = jnp.exp(s - m)
        l = jnp.sum(p, axis=-1, keepdims=True)
        acc = _dot(p.astype(BF16), v_ref[pl.ds(d0, blk), hcols(h)].astype(BF16))
        qss.append(qs)
        init.append((m, l, acc))

    def body(t, carry):
        j = kb * t
        j0 = pl.multiple_of(j * blk, kb * blk)
        out = []
        for h in range(hp):
            m, l, acc = carry[h]
            kj = k_ref[pl.ds(j0, kb * blk), hcols(h)].astype(BF16)
            vj = v_ref[pl.ds(j0, kb * blk), hcols(h)].astype(BF16)
            bias = jnp.concatenate([bias_ref[h, j + jj] for jj in range(kb) for _ in range(blk // LANE)], axis=-1)
            s = _dot_t(qss[h], kj) + bias
            m_new = jnp.maximum(m, jnp.max(s, axis=-1, keepdims=True))
            alpha = jnp.exp(m - m_new)
            p = jnp.exp(s - m_new)
            l = alpha * l + jnp.sum(p, axis=-1, keepdims=True)
            acc = alpha * acc + _dot(p.astype(BF16), vj)
            out.append((m_new, l, acc))
        return tuple(out)

    fin = lax.fori_loop(0, (qi + kb - 1) // kb, body, tuple(init))
    o_ref[...] = jnp.concatenate([acc / l for (_, l, acc) in fin], axis=-1)


def moba_prompt_attn(z, n_seq, seq_len):
    blk = MOBA_BLOCK
    nblk = seq_len // blk
    hp, kb = MOBA_HEADS_PER_STEP, MOBA_BLOCKS_PER_ITER
    assert nblk % kb == 0 and H_A % hp == 0 and nblk < LANE
    w = hp * D_HEAD
    row = jnp.arange(LANE)[:, None]
    slab = jnp.arange(nblk * LANE)[None, :] // LANE
    spread = jnp.where(row == slab, MASK_BIG, jnp.where(row == LANE - 1, -MASK_BIG, 0.0)).astype(BF16)
    return pl.pallas_call(
        functools.partial(_moba_prompt_kernel, nblk=nblk, hp=hp, kb=kb),
        grid=(n_seq, H_A // hp, nblk),
        in_specs=[
            pl.BlockSpec((blk, w), lambda n, h, i: (n * nblk + i, CB_QA // hp + h)),
            pl.BlockSpec((seq_len, w), lambda n, h, i: (n, CB_KA // hp + h)),
            pl.BlockSpec((seq_len, w), lambda n, h, i: (n, CB_VA // hp + h)),
            pl.BlockSpec((LANE, nblk * LANE), lambda n, h, i: (0, 0)),
        ],
        out_specs=pl.BlockSpec((blk, w), lambda n, h, i: (n * nblk + i, h)),
        out_shape=jax.ShapeDtypeStruct((n_seq * seq_len, W_A), F32),
        scratch_shapes=[pltpu.VMEM((hp, nblk, D_HEAD), F32), pltpu.VMEM((hp, nblk, blk, LANE), F32)],
        compiler_params=_params("parallel", "parallel", "arbitrary"),
        name="moba_prompt",
    )(z, z, z, spread)


def _sb_block(zl, c, upper, mask):
    sp = _softplus(zl)
    lk = -sp
    if mask is not None:
        lk = jnp.where(mask, lk, 0.0)
    hi, lo = _split_bf16(lk)
    after = _dot(hi, upper) + _dot(lo, upper)
    a = jnp.exp((zl - sp) + after + c)
    if mask is not None:
        a = jnp.where(mask, a, 0.0)
    return a, c + after[:, 0:1] + lk[:, 0:1]


def _upper(n):
    r_i = lax.broadcasted_iota(jnp.int32, (n, n), 0)
    c_i = lax.broadcasted_iota(jnp.int32, (n, n), 1)
    return jnp.where(r_i > c_i, 1.0, 0.0).astype(BF16), c_i < r_i


def _sb_prompt_kernel(q_ref, k_ref, v_ref, o_ref, *, blk):
    qi = pl.program_id(2)
    scale = D_HEAD ** -0.5
    qs = (q_ref[...] * scale).astype(BF16)
    upper, causal = _upper(blk)

    def block(j, c, acc, mask):
        j0 = pl.multiple_of(j * blk, blk)
        kj = k_ref[pl.ds(j0, blk), :].astype(BF16)
        vj = v_ref[pl.ds(j0, blk), :].astype(BF16)
        a, c = _sb_block(_dot_t(qs, kj), c, upper, mask)
        return c, acc + _dot(a.astype(BF16), vj)

    c, acc = block(qi, jnp.zeros((blk, 1), F32), jnp.zeros((blk, D_HEAD), F32), causal)

    def cond(carry):
        t, c, _ = carry
        return jnp.logical_and(t < qi, jnp.max(c) > SB_LOG_FLOOR)

    def body(carry):
        t, c, acc = carry
        c, acc = block(qi - 1 - t, c, acc, None)
        return t + 1, c, acc

    _, c, acc = lax.while_loop(cond, body, (jnp.int32(0), c, acc))
    o_ref[...] = acc


def sb_prompt_attn(z, n_seq, seq_len, *, blk=256):
    nblk = seq_len // blk
    return pl.pallas_call(
        functools.partial(_sb_prompt_kernel, blk=blk),
        grid=(n_seq, H_B, nblk),
        in_specs=[
            pl.BlockSpec((blk, D_HEAD), lambda n, h, i: (n * nblk + i, CB_QB + h)),
            pl.BlockSpec((seq_len, D_HEAD), lambda n, h, i: (n, CB_KB + h)),
            pl.BlockSpec((seq_len, D_HEAD), lambda n, h, i: (n, CB_VB + h)),
        ],
        out_specs=pl.BlockSpec((blk, D_HEAD), lambda n, h, i: (n * nblk + i, h)),
        out_shape=jax.ShapeDtypeStruct((n_seq * seq_len, W_B), F32),
        compiler_params=_params("parallel", "parallel", "arbitrary"),
        name="sb_prompt",
    )(z, z, z)


def _page_head(ref, h, n_heads):
    return ref[0, 0, pl.ds(h, PAGE_SIZE, stride=n_heads), :]


def _stack_heads(x, n_heads):
    return jnp.concatenate([_head_cols(x, h) for h in range(n_heads)], axis=0)


def _unstack_heads(x, n_heads, t):
    return jnp.concatenate([x[h * t:(h + 1) * t, :] for h in range(n_heads)], axis=-1)


def _moba_sample_kernel(pt_ref, q_ref, kn_ref, vn_ref, *refs, t, n_pages, pg):
    k_refs = refs[:pg]
    v_refs = refs[pg:2 * pg]
    o_ref = refs[2 * pg]
    s_ref, p_ref, ksum_ref, acc_ref, l_ref = refs[2 * pg + 1:]
    del pt_ref
    s = pl.program_id(1)
    ng = n_pages // pg
    ppb = MOBA_BLOCK // PAGE_SIZE
    nblk = n_pages // ppb
    scale = D_HEAD ** -0.5
    rows = H_A * t
    qf = _stack_heads(q_ref[...], H_A)
    qb = (qf * scale).astype(BF16)
    r_i = lax.broadcasted_iota(jnp.int32, (rows, PAGE_SIZE * H_A), 0)
    c_i = lax.broadcasted_iota(jnp.int32, (rows, PAGE_SIZE * H_A), 1)
    head_bias = jnp.where(c_i % H_A == r_i // t, 0.0, NEG)

    @pl.when(s < ng)
    def _():
        for i in range(pg):
            kp = k_refs[i][0, 0]
            ksum_ref[s * pg + i] = jnp.sum(kp.reshape(PAGE_SIZE, H_A, D_HEAD), axis=0)
            s_ref[s * pg + i] = _dot_t(qb, kp.astype(BF16)) + head_bias

    @pl.when(s == ng - 1)
    def _():
        sc = []
        for j in range(nblk):
            mean_j = sum(ksum_ref[j * ppb + i] for i in range(ppb)) * (1.0 / MOBA_BLOCK)
            mexp = jnp.concatenate([jnp.broadcast_to(mean_j[h:h + 1, :], (t, D_HEAD)) for h in range(H_A)], axis=0)
            sc.append(jnp.sum(qf * mexp, axis=-1, keepdims=True))
        bias = []
        for j in range(nblk):
            rank = sum(((sc[i] >= sc[j]) if i < j else (sc[i] > sc[j])).astype(jnp.int32)
                       for i in range(nblk) if i != j)
            bias.append(jnp.where(rank < MOBA_TOPK, 0.0, NEG))
        lo = _dot_t(qb, _pad_rows(_stack_heads(kn_ref[...], H_A), LANE).astype(BF16))
        r_o = lax.broadcasted_iota(jnp.int32, lo.shape, 0)
        c_o = lax.broadcasted_iota(jnp.int32, lo.shape, 1)
        lo = jnp.where((c_o // t == r_o // t) & (c_o % t <= r_o % t), lo, NEG)
        m = jnp.max(lo, axis=-1, keepdims=True)
        for pi in range(n_pages):
            m = jnp.maximum(m, jnp.max(s_ref[pi] + bias[pi // ppb], axis=-1, keepdims=True))
        po = jnp.exp(lo - m)
        l = jnp.sum(po, axis=-1, keepdims=True)
        for pi in range(n_pages):
            pp = jnp.exp(s_ref[pi] + bias[pi // ppb] - m)
            l = l + jnp.sum(pp, axis=-1, keepdims=True)
            p_ref[pi] = pp.astype(BF16)
        l_ref[...] = jnp.broadcast_to(l, l_ref.shape)
        acc_ref[...] = _dot(po.astype(BF16), _pad_rows(_stack_heads(vn_ref[...], H_A), LANE).astype(BF16))

    @pl.when(s >= ng)
    def _():
        for i in range(pg):
            acc_ref[...] += _dot(p_ref[(s - ng) * pg + i], v_refs[i][0, 0].astype(BF16))

    @pl.when(s == 2 * ng - 1)
    def _():
        o_ref[...] = _unstack_heads(acc_ref[...] / l_ref[:, 0:1], H_A, t)


def moba_sample_attn(z, pool_k, pool_v, page_table, layer, n_dec, t, *, pg=8):
    n_pages = page_table.shape[1]
    ng = n_pages // pg
    rows = H_A * t
    zspec = lambda cb: pl.BlockSpec((t, W_A), lambda b, s, pt: (b, cb * LANE // W_A))
    page_block = (1, 1, PAGE_SIZE * H_A, D_HEAD)
    kspec = lambda i: pl.BlockSpec(
        page_block, lambda b, s, pt: (layer, pt[b, jnp.minimum(s, ng - 1) * pg + i], 0, 0))
    vspec = lambda i: pl.BlockSpec(
        page_block, lambda b, s, pt: (layer, pt[b, jnp.maximum(s - ng, 0) * pg + i], 0, 0))
    grid_spec = pltpu.PrefetchScalarGridSpec(
        num_scalar_prefetch=1,
        grid=(n_dec, 2 * ng),
        in_specs=[zspec(CB_QA), zspec(CB_KA), zspec(CB_VA)] + [kspec(i) for i in range(pg)] + [vspec(i) for i in range(pg)],
        out_specs=pl.BlockSpec((t, W_A), lambda b, s, pt: (b, 0)),
        scratch_shapes=[
            pltpu.VMEM((n_pages, rows, PAGE_SIZE * H_A), F32), pltpu.VMEM((n_pages, rows, PAGE_SIZE * H_A), BF16),
            pltpu.VMEM((n_pages, H_A, D_HEAD), F32),
            pltpu.VMEM((rows, D_HEAD), F32), pltpu.VMEM((rows, LANE), F32)],
    )
    return pl.pallas_call(
        functools.partial(_moba_sample_kernel, t=t, n_pages=n_pages, pg=pg),
        grid_spec=grid_spec,
        out_shape=jax.ShapeDtypeStruct((n_dec * t, W_A), F32),
        compiler_params=_params("parallel", "arbitrary"),
        name="moba_sample",
    )(page_table, z, z, z, *([pool_k] * pg), *([pool_v] * pg))


def _sb_visitor(q_ref, acc_ref, c_ref, live_ref, t):
    scale = D_HEAD ** -0.5
    upper, _ = _upper(PAGE_SIZE)
    hrows = lambda h: slice(h * t, (h + 1) * t)
    qs = _stack_heads(q_ref[...], H_B) * scale

    def visit(k_of, v_of, mask):
        zl = jnp.concatenate([_dot_t(qs[hrows(h)].astype(BF16), k_of(h)) for h in range(H_B)], axis=0)
        a, c = _sb_block(zl, c_ref[:, 0:1], upper, mask)
        acc_ref[...] += jnp.concatenate([_dot(a[hrows(h)].astype(BF16), v_of(h)) for h in range(H_B)], axis=0)
        c_ref[...] = jnp.broadcast_to(c, c_ref.shape)
        live_ref[0] = (jnp.max(c) > SB_LOG_FLOOR).astype(jnp.int32)

    return visit


def _visit_pages(visit, k_refs, v_refs, live_ref):
    for k_ref, v_ref in zip(k_refs, v_refs):
        @pl.when(live_ref[0] > 0)
        def _(k_ref=k_ref, v_ref=v_ref):
            visit(lambda h: _page_head(k_ref, h, H_B).astype(BF16),
                  lambda h: _page_head(v_ref, h, H_B).astype(BF16), None)


def _sb_sample_head_kernel(pt_ref, q_ref, kn_ref, vn_ref, *refs, t, pg):
    k_refs, v_refs = refs[:pg], refs[pg:2 * pg]
    acc_ref, c_ref, flag_ref, live_ref = refs[2 * pg:]
    del pt_ref
    visit = _sb_visitor(q_ref, acc_ref, c_ref, live_ref, t)
    acc_ref[...] = jnp.zeros_like(acc_ref)
    c_ref[...] = jnp.zeros_like(c_ref)
    kn, vn = kn_ref[...], vn_ref[...]
    r_t = lax.broadcasted_iota(jnp.int32, (H_B * t, PAGE_SIZE), 0) % t
    c_j = lax.broadcasted_iota(jnp.int32, (H_B * t, PAGE_SIZE), 1)
    visit(lambda h: _pad_rows(_head_cols(kn, h), PAGE_SIZE).astype(BF16),
          lambda h: _pad_rows(_head_cols(vn, h), PAGE_SIZE).astype(BF16), c_j < r_t)
    _visit_pages(visit, k_refs, v_refs, live_ref)
    flag_ref[...] = jnp.full(flag_ref.shape, live_ref[0], jnp.int32)


def _sb_sample_tail_kernel(pt_ref, flag_s_ref, q_ref, acc_in_ref, c_in_ref, *refs, t, pg):
    k_refs, v_refs = refs[:pg], refs[pg:2 * pg]
    o_ref, acc_ref, c_ref, live_ref = refs[2 * pg:]
    del pt_ref
    s = pl.program_id(1)

    @pl.when(s == 0)
    def _():
        acc_ref[...] = acc_in_ref[...]
        c_ref[...] = c_in_ref[...]
        live_ref[0] = flag_s_ref[pl.program_id(0)]

    @pl.when(live_ref[0] > 0)
    def _():
        visit = _sb_visitor(q_ref, acc_ref, c_ref, live_ref, t)
        _visit_pages(visit, k_refs, v_refs, live_ref)

    @pl.when(s == pl.num_programs(1) - 1)
    def _():
        o_ref[...] = _unstack_heads(acc_ref[...], H_B, t)


def sb_sample_attn(z, pool_k, pool_v, page_table, layer, n_dec, t, *, head_pages=2, tail_pg=7):
    n_pages = page_table.shape[1]
    head_pages = min(head_pages, n_pages)
    rows = H_B * t
    page_block = (1, 1, PAGE_SIZE * H_B, D_HEAD)
    acc_spec = lambda *_: pl.BlockSpec((rows, LANE), lambda b, *_: (b, 0))
    head_page = lambda i: pl.BlockSpec(page_block, lambda b, pt: (layer, pt[b, n_pages - 1 - i], 0, 0))
    acc, c, flag = pl.pallas_call(
        functools.partial(_sb_sample_head_kernel, t=t, pg=head_pages),
        grid_spec=pltpu.PrefetchScalarGridSpec(
            num_scalar_prefetch=1,
            grid=(n_dec,),
            in_specs=[pl.BlockSpec((t, W_B), lambda b, pt, cb=cb: (b, cb * LANE // W_B)) for cb in (CB_QB, CB_KB, CB_VB)]
            + [head_page(i) for i in range(head_pages)] * 2,
            out_specs=[acc_spec(), acc_spec(), pl.BlockSpec((8, LANE), lambda b, pt: (b, 0))],
            scratch_shapes=[pltpu.SMEM((1,), jnp.int32)],
        ),
        out_shape=[jax.ShapeDtypeStruct((n_dec * rows, LANE), F32), jax.ShapeDtypeStruct((n_dec * rows, LANE), F32),
                   jax.ShapeDtypeStruct((n_dec * 8, LANE), jnp.int32)],
        compiler_params=_params("parallel"),
        name="sb_sample_head",
    )(page_table, z, z, z, *([pool_k] * head_pages), *([pool_v] * head_pages))
    n_tail = n_pages - head_pages
    assert n_tail > 0
    tail_pg = math.gcd(n_tail, tail_pg)
    steps = n_tail // tail_pg
    live = flag[::8, 0]

    def tail_page(i):
        def index(b, s, pt, lv):
            pos = n_pages - 1 - (head_pages + s * tail_pg + i)
            return (layer, jnp.where(lv[b] > 0, pt[b, pos], 0), 0, 0)
        return pl.BlockSpec(page_block, index)

    return pl.pallas_call(
        functools.partial(_sb_sample_tail_kernel, t=t, pg=tail_pg),
        grid_spec=pltpu.PrefetchScalarGridSpec(
            num_scalar_prefetch=2,
            grid=(n_dec, steps),
            in_specs=[pl.BlockSpec((t, W_B), lambda b, s, pt, lv: (b, CB_QB * LANE // W_B)),
                      pl.BlockSpec((rows, LANE), lambda b, s, pt, lv: (b, 0)),
                      pl.BlockSpec((rows, LANE), lambda b, s, pt, lv: (b, 0))]
            + [tail_page(i) for i in range(tail_pg)] * 2,
            out_specs=pl.BlockSpec((t, W_B), lambda b, s, pt, lv: (b, 0)),
            scratch_shapes=[pltpu.VMEM((rows, LANE), F32), pltpu.VMEM((rows, LANE), F32), pltpu.SMEM((1,), jnp.int32)],
        ),
        out_shape=jax.ShapeDtypeStruct((n_dec * t, W_B), F32),
        compiler_params=_params("parallel", "arbitrary"),
        name="sb_sample_tail",
    )(page_table, live, z, acc, c, *([pool_k] * tail_pg), *([pool_v] * tail_pg))


def _gla_kernel(q_ref, k_ref, v_ref, rc_ref, w2_ref, b_ref, s0_ref, o_ref, sfin_ref, st_ref, *, c, nchunk):
    step = pl.program_id(1)

    @pl.when(step == 0)
    def _():
        st_ref[...] = jnp.zeros_like(st_ref)
        for h in range(H_C):
            st_ref[h * DK_C:(h + 1) * DK_C, h * DV_C:(h + 1) * DV_C] = s0_ref[0, h]

    r_i = lax.broadcasted_iota(jnp.int32, (c, c), 0)
    c_i = lax.broadcasted_iota(jnp.int32, (c, c), 1)
    tril = jnp.where(c_i <= r_i, 1.0, 0.0).astype(F32)
    s_idx = lax.broadcasted_iota(jnp.int32, (c, WK_C), 0)
    kh = lax.broadcasted_iota(jnp.int32, (WK_C, WV_C), 0) // DK_C
    vh = lax.broadcasted_iota(jnp.int32, (WK_C, WV_C), 1) // DV_C
    diag = kh == vh
    expand = jnp.where(diag, 1.0, 0.0).astype(BF16)
    pad_row = lax.broadcasted_iota(jnp.int32, (LANE, WK_C), 0)
    pad_col = lax.broadcasted_iota(jnp.int32, (WK_C, LANE), 1)

    def chunk(ci, carry):
        r0 = pl.multiple_of(ci * c, c)
        q = q_ref[pl.ds(r0, c), :] * (DK_C ** -0.5)
        k = k_ref[pl.ds(r0, c), :]
        v = v_ref[pl.ds(r0, c), :]
        x = _dot(rc_ref[pl.ds(r0, c), :].astype(BF16), w2_ref[...]) + b_ref[...]
        la = (jnp.minimum(x, 0.0) - jnp.log(1.0 + jnp.exp(-jnp.abs(x)))) * (1.0 / GLA_GATE_NORM)
        bc = _dot(tril, la, precision=HIGHEST)
        ps = []
        for t in range(c):
            diff = bc[t:t + 1, :] - bc
            dec = jnp.exp(jnp.where(s_idx <= t, diff, -jnp.inf))
            ps.append((q[t:t + 1, :] * dec) * k)
        a = _dot(jnp.concatenate(ps, axis=0).astype(BF16), expand)
        o_rows = [jnp.sum(a[t * c:(t + 1) * c, :] * v, axis=0, keepdims=True) for t in range(c)]
        st = st_ref[...]
        o = jnp.concatenate(o_rows, axis=0) + _dot((q * jnp.exp(bc)).astype(BF16), st.astype(BF16))
        o_ref[pl.ds(r0, c), :] = o
        b_last = bc[c - 1:c, :]
        kd = k * jnp.exp(b_last - bc)
        packed = jnp.where(pad_row == c, jnp.broadcast_to(b_last, (LANE, WK_C)), _pad_rows(kd, LANE))
        packed_t = packed.T
        g_col = jnp.exp(packed_t[:, c:c + 1])
        kd_t = jnp.where(pad_col < c, packed_t, 0.0).astype(BF16)
        ds = _dot(kd_t, _pad_rows(v, LANE).astype(BF16))
        st_ref[...] = st * g_col + jnp.where(diag, ds, 0.0)
        return carry

    lax.fori_loop(0, nchunk, chunk, 0, unroll=min(2, nchunk))

    @pl.when(step == pl.num_programs(1) - 1)
    def _():
        for h in range(H_C):
            sfin_ref[0, h] = st_ref[h * DK_C:(h + 1) * DK_C, h * DV_C:(h + 1) * DV_C]


def gla(z, w2, b, s0, n, length, *, tb=256):
    c = math.gcd(length, GLA_CHUNK)
    tb = math.gcd(length, tb)
    steps = length // tb
    w2p = jnp.concatenate([w2, jnp.zeros((LANE - GLA_RANK, WK_C), w2.dtype)], axis=0).astype(BF16)
    row = lambda i, s: i * steps + s
    return pl.pallas_call(
        functools.partial(_gla_kernel, c=c, nchunk=tb // c),
        grid=(n, steps),
        in_specs=[
            pl.BlockSpec((tb, WK_C), lambda i, s: (row(i, s), CB_QC * LANE // WK_C)),
            pl.BlockSpec((tb, WK_C), lambda i, s: (row(i, s), CB_KC * LANE // WK_C)),
            pl.BlockSpec((tb, WV_C), lambda i, s: (row(i, s), CB_VC * LANE // WV_C)),
            pl.BlockSpec((tb, LANE), lambda i, s: (row(i, s), CB_RC)),
            pl.BlockSpec((LANE, WK_C), lambda i, s: (0, 0)),
            pl.BlockSpec((1, WK_C), lambda i, s: (0, 0)),
            pl.BlockSpec((1, H_C, DK_C, DV_C), lambda i, s: (i, 0, 0, 0)),
        ],
        out_specs=[
            pl.BlockSpec((tb, WV_C), lambda i, s: (row(i, s), 0)),
            pl.BlockSpec((1, H_C, DK_C, DV_C), lambda i, s: (i, 0, 0, 0)),
        ],
        out_shape=[jax.ShapeDtypeStruct((n * length, WV_C), F32), jax.ShapeDtypeStruct((n, H_C, DK_C, DV_C), F32)],
        scratch_shapes=[pltpu.VMEM((WK_C, WV_C), F32)],
        compiler_params=_params("parallel", "arbitrary"),
        name="gla",
    )(z, z, z, z, w2p, b.reshape(1, WK_C), s0)


def _branch_kernel(ma_ref, mb_ref, mc_ref, ga_ref, gb_ref, gc_ref, oa_ref, ob_ref, oc_ref, gain_ref,
                   wa_ref, wb_ref, wc_ref, m_ref):
    silu = lambda g: g * _sigmoid(g)
    ya = _dot((oa_ref[...] * silu(ga_ref[...])).astype(BF16), wa_ref[...])
    yb = _dot((ob_ref[...] * silu(gb_ref[...])).astype(BF16), wb_ref[...])
    oc = oc_ref[...]
    gain = gain_ref[...]
    heads = []
    for h in range(H_C):
        oh = oc[:, h * DV_C:(h + 1) * DV_C]
        heads.append(oh * lax.rsqrt(jnp.mean(oh * oh, axis=-1, keepdims=True) + RMS_EPS) * gain)
    ocn = jnp.concatenate(heads, axis=-1)
    yc = _dot((ocn * silu(gc_ref[...])).astype(BF16), wc_ref[...])
    m = _sigmoid(ma_ref[...]) * ya + _sigmoid(mb_ref[...]) * yb + _sigmoid(mc_ref[...]) * yc
    m_ref[...] = m.astype(BF16)


def branch_merge(z, oa, ob, oc, gla_gain, wa, wb, wc, layer, *, tm=256):
    m = z.shape[0]
    tm = math.gcd(m, tm)
    zb = lambda cb, w: pl.BlockSpec((tm, w), lambda i: (i, cb * LANE // w))
    rows = lambda w: pl.BlockSpec((tm, w), lambda i: (i, 0))
    wspec = lambda w: pl.BlockSpec((None, w, D_MODEL), lambda i: (layer, 0, 0))
    return pl.pallas_call(
        _branch_kernel,
        grid=(m // tm,),
        in_specs=[zb(CB_MA, D_MODEL), zb(CB_MB, D_MODEL), zb(CB_MC, D_MODEL),
                  zb(CB_GA, W_A), zb(CB_GB, W_B), zb(CB_GC, WV_C),
                  rows(W_A), rows(W_B), rows(WV_C), pl.BlockSpec((1, DV_C), lambda i: (0, 0)),
                  wspec(W_A), wspec(W_B), wspec(WV_C)],
        out_specs=rows(D_MODEL),
        out_shape=jax.ShapeDtypeStruct((m, D_MODEL), BF16),
        compiler_params=_params("parallel"),
        name="branch_merge",
    )(z, z, z, z, z, z, oa, ob, oc, gla_gain.reshape(1, DV_C), wa, wb, wc)


def _outproj_kernel(x_ref, m_ref, w_ref, g_ref, o_ref, *, final):
    y = x_ref[...] + _dot(m_ref[...], w_ref[...])
    if final:
        y = y * lax.rsqrt(jnp.mean(y * y, axis=-1, keepdims=True) + RMS_EPS) * g_ref[...]
    o_ref[...] = y


def out_project(x, m, w_out, final_gain, layer, *, final, tm=512):
    rows, d = x.shape
    tm = math.gcd(rows, tm)
    return pl.pallas_call(
        functools.partial(_outproj_kernel, final=final),
        grid=(rows // tm,),
        in_specs=[pl.BlockSpec((tm, d), lambda i: (i, 0)), pl.BlockSpec((tm, d), lambda i: (i, 0)),
                  pl.BlockSpec((None, d, d), lambda i: (layer, 0, 0)), pl.BlockSpec((1, d), lambda i: (0, 0))],
        out_specs=pl.BlockSpec((tm, d), lambda i: (i, 0)),
        out_shape=jax.ShapeDtypeStruct((rows, d), F32),
        compiler_params=_params("parallel"),
        name="out_project",
    )(x, m, w_out, final_gain.reshape(1, d))


def _kv_writer_kernel(*refs, heads, n_layers):
    nseg = len(heads)
    ins, outs = refs[:n_layers * nseg], refs[n_layers * nseg:]
    layer = pl.program_id(0)
    for li in range(n_layers):
        @pl.when(layer == li)
        def _(li=li):
            for sg, nh in enumerate(heads):
                src, dst = ins[li * nseg + sg], outs[sg]
                for h in range(nh):
                    dst[0, pl.ds(h, src.shape[0], stride=nh), :] = src[:, h * D_HEAD:(h + 1) * D_HEAD]


def kv_rows(zs, *, tm=512):
    n_layers = len(zs)
    rows = zs[0].shape[0]
    tm = math.gcd(rows, tm)
    nt = rows // tm
    segs = ((CB_KA, H_A), (CB_VA, H_A), (CB_KB, H_B), (CB_VB, H_B))
    heads = tuple(nh for _, nh in segs)

    def in_spec(li, cb, nh):
        w = nh * D_HEAD
        return pl.BlockSpec((tm, w), lambda l, i: (jnp.clip(i + (l - li) * nt, 0, nt - 1), cb * LANE // w))

    return pl.pallas_call(
        functools.partial(_kv_writer_kernel, heads=heads, n_layers=n_layers),
        grid=(n_layers, nt),
        in_specs=[in_spec(li, cb, nh) for li in range(n_layers) for cb, nh in segs],
        out_specs=[pl.BlockSpec((1, tm * nh, D_HEAD), lambda l, i: (l, i, 0)) for nh in heads],
        out_shape=[jax.ShapeDtypeStruct((n_layers, rows * nh, D_HEAD), F32) for nh in heads],
        compiler_params=_params("arbitrary", "arbitrary"),
        name="kv_rows",
    )(*[z for z in zs for _ in segs])


def kernel(x_prompt, x_sample, cache_moba_k, cache_moba_v, cache_sb_k, cache_sb_v, state_gla, page_table,
           norm_gain, w_in, gla_w2, gla_b, gla_out_gain, w_br_a, w_br_b, w_br_c, w_out, final_gain):
    n_seq, seq_len, _ = x_prompt.shape
    n_dec, dec_len, _ = x_sample.shape
    n_pool = cache_moba_k.shape[1]
    x_p = x_prompt.reshape(n_seq * seq_len, D_MODEL)
    x_s = x_sample.reshape(n_dec * dec_len, D_MODEL)
    w_all = jnp.concatenate(
        [w_in[:, :, MAIN_W + GLA_RANK:], w_in[:, :, :MAIN_W], w_in[:, :, MAIN_W:MAIN_W + GLA_RANK],
         jnp.zeros((DEPTH, D_MODEL, RC_PAD - GLA_RANK), w_in.dtype)], axis=-1).astype(BF16)
    wa, wb, wc, wo = (w.astype(BF16) for w in (w_br_a, w_br_b, w_br_c, w_out))
    pool_ak = cache_moba_k.reshape(DEPTH, n_pool, PAGE_SIZE * H_A, D_HEAD)
    pool_av = cache_moba_v.reshape(DEPTH, n_pool, PAGE_SIZE * H_A, D_HEAD)
    pool_bk = cache_sb_k.reshape(DEPTH, n_pool, PAGE_SIZE * H_B, D_HEAD)
    pool_bv = cache_sb_v.reshape(DEPTH, n_pool, PAGE_SIZE * H_B, D_HEAD)
    zero_state = jnp.zeros((n_seq, H_C, DK_C, DV_C), F32)

    zs_p, zs_s, st_p, st_s = [], [], [], []
    for l in range(DEPTH):
        last = l == DEPTH - 1
        z = norm_project(x_p, norm_gain[l], w_all, l)
        oa = moba_prompt_attn(z, n_seq, seq_len)
        ob = sb_prompt_attn(z, n_seq, seq_len)
        oc, sf = gla(z, gla_w2[l], gla_b[l], zero_state, n_seq, seq_len)
        m = branch_merge(z, oa, ob, oc, gla_out_gain[l], wa, wb, wc, l)
        x_p = out_project(x_p, m, wo, final_gain, l, final=last)
        zs_p.append(z)
        st_p.append(sf)
        z = norm_project(x_s, norm_gain[l], w_all, l)
        oa = moba_sample_attn(z, pool_ak, pool_av, page_table, l, n_dec, dec_len)
        ob = sb_sample_attn(z, pool_bk, pool_bv, page_table, l, n_dec, dec_len)
        oc, sf = gla(z, gla_w2[l], gla_b[l], state_gla[l].astype(F32), n_dec, dec_len)
        m = branch_merge(z, oa, ob, oc, gla_out_gain[l], wa, wb, wc, l)
        x_s = out_project(x_s, m, wo, final_gain, l, final=last)
        zs_s.append(z)
        st_s.append(sf)

    kv_p = kv_rows(zs_p)
    kv_s = kv_rows(zs_s)
    shape_p = lambda a, nh: a.reshape(DEPTH, n_seq, seq_len, nh, D_HEAD)
    shape_s = lambda a, nh: a.reshape(DEPTH, n_dec, dec_len, nh, D_HEAD)
    return (x_p.reshape(n_seq, seq_len, D_MODEL), x_s.reshape(n_dec, dec_len, D_MODEL),
            shape_p(kv_p[0], H_A), shape_p(kv_p[1], H_A), shape_p(kv_p[2], H_B), shape_p(kv_p[3], H_B),
            jnp.stack(st_p),
            shape_s(kv_s[0], H_A), shape_s(kv_s[1], H_A), shape_s(kv_s[2], H_B), shape_s(kv_s[3], H_B),
            jnp.stack(st_s))
```

```python
import functools
import math

import jax
import jax.numpy as jnp
from jax import lax
from jax.experimental import pallas as pl
from jax.experimental.pallas import tpu as pltpu

D_MODEL = 2048
DEPTH = 2
PAGE_SIZE = 128
D_HEAD = 128
H_A = 8
H_B = 4
H_C = 4
DK_C = 64
DV_C = 128
W_A = H_A * D_HEAD
W_B = H_B * D_HEAD
WK_C = H_C * DK_C
WV_C = H_C * DV_C
GLA_RANK = 16
GLA_GATE_NORM = 16.0
GLA_CHUNK = 16
MOBA_BLOCK = 256
MOBA_TOPK = 3
RMS_EPS = 1e-6

LANE = 128
MAIN_W = 4 * W_A + 4 * W_B + 2 * WK_C + 2 * WV_C
GATE_W = 3 * D_MODEL
RC_PAD = 512
Z_W = GATE_W + MAIN_W + RC_PAD
CB_MA, CB_MB, CB_MC = 0, 16, 32
CB_QA, CB_KA, CB_VA, CB_GA = 48, 56, 64, 72
CB_QB, CB_KB, CB_VB, CB_GB = 80, 84, 88, 92
CB_QC, CB_KC, CB_VC, CB_GC = 96, 98, 100, 104
CB_RC = 108

NEG = -1e30
SB_LOG_FLOOR = -104.0
VMEM_LIMIT = 48 * 1024 * 1024
MOBA_HEADS_PER_STEP = 2
MOBA_BLOCKS_PER_ITER = 2
MASK_BIG = 2.0 ** 100

F32 = jnp.float32
BF16 = jnp.bfloat16
HIGHEST = lax.Precision.HIGHEST


def _dot_t(a, b, **kw):
    return lax.dot_general(a, b, (((1,), (1,)), ((), ())), preferred_element_type=F32, **kw)


def _dot(a, b, **kw):
    return jnp.dot(a, b, preferred_element_type=F32, **kw)


def _softplus(z):
    return jnp.maximum(z, 0.0) + jnp.log(1.0 + jnp.exp(-jnp.abs(z)))


def _sigmoid(z):
    return 1.0 / (1.0 + jnp.exp(-z))


def _split_bf16(x):
    hi = x.astype(BF16)
    lo = (x - hi.astype(F32)).astype(BF16)
    return hi, lo


def _params(*sem, vmem=VMEM_LIMIT):
    return pltpu.CompilerParams(dimension_semantics=sem, vmem_limit_bytes=vmem)


def _pad_rows(x, rows):
    return jnp.concatenate([x, jnp.zeros((rows - x.shape[0], x.shape[1]), x.dtype)], axis=0)


def _head_cols(x, h):
    return x[:, h * D_HEAD:(h + 1) * D_HEAD]


def _proj_kernel(x_ref, g_ref, w_ref, o_ref, xn_ref):
    @pl.when(pl.program_id(1) == 0)
    def _():
        x = x_ref[...]
        ms = jnp.mean(x * x, axis=-1, keepdims=True)
        xn_ref[...] = (x * lax.rsqrt(ms + RMS_EPS) * g_ref[...]).astype(BF16)

    o_ref[...] = _dot(xn_ref[...], w_ref[...])


def norm_project(x, gain, w_bf16, layer, *, tm=1024, tn=1024):
    m, d = x.shape
    n = w_bf16.shape[2]
    tm = math.gcd(m, tm)
    return pl.pallas_call(
        _proj_kernel,
        grid=(m // tm, n // tn),
        in_specs=[
            pl.BlockSpec((tm, d), lambda i, j: (i, 0)),
            pl.BlockSpec((1, d), lambda i, j: (0, 0)),
            pl.BlockSpec((None, d, tn), lambda i, j: (layer, 0, j)),
        ],
        out_specs=pl.BlockSpec((tm, tn), lambda i, j: (i, j)),
        out_shape=jax.ShapeDtypeStruct((m, n), F32),
        scratch_shapes=[pltpu.VMEM((tm, d), BF16)],
        compiler_params=_params("parallel", "arbitrary"),
        name="norm_project",
    )(x, gain.reshape(1, d), w_bf16)


def _topk_select_t(sc_t, allowed, nblk):
    blk_i = lax.broadcasted_iota(jnp.int32, sc_t.shape, 0)
    sc_t = jnp.where(allowed, sc_t, -jnp.inf)
    rank = jnp.zeros(sc_t.shape, jnp.int32)
    for i in range(nblk):
        si = sc_t[i:i + 1, :]
        ge = (si >= sc_t).astype(jnp.int32)
        gt = (si > sc_t).astype(jnp.int32)
        rank = rank + jnp.where(blk_i > i, ge, gt)
    return jnp.where(allowed, (rank < MOBA_TOPK).astype(F32), 0.0)


def _moba_prompt_kernel(q_ref, k_ref, v_ref, spread_ref, o_ref, means_ref, bias_ref, *, nblk, hp, kb):
    qi = pl.program_id(2)
    blk = MOBA_BLOCK
    scale = D_HEAD ** -0.5
    hcols = lambda h: slice(h * D_HEAD, (h + 1) * D_HEAD)

    @pl.when(qi == 0)
    def _():
        for h in range(hp):
            for j in range(nblk):
                means_ref[h, j:j + 1, :] = (
                    jnp.sum(k_ref[j * blk:(j + 1) * blk, hcols(h)], axis=0, keepdims=True) * (1.0 / blk))

    d0 = pl.multiple_of(qi * blk, blk)
    r_i = lax.broadcasted_iota(jnp.int32, (blk, blk), 0)
    c_i = lax.broadcasted_iota(jnp.int32, (blk, blk), 1)
    qss, init = [], []
    for h in range(hp):
        q = q_ref[:, hcols(h)]
        sc_t = _dot_t(means_ref[h], q, precision=HIGHEST)
        blk_i = lax.broadcasted_iota(jnp.int32, sc_t.shape, 0)
        sel_t = _topk_select_t(sc_t, blk_i < qi, nblk)
        pad = jnp.where(lax.broadcasted_iota(jnp.int32, (LANE - nblk, blk), 0) == LANE - nblk - 1, 1.0, 0.0)
        sel = jnp.concatenate([sel_t, pad], axis=0).T.astype(BF16)
        bias = _dot(sel, spread_ref[...])
        for j in range(nblk):
            bias_ref[h, j] = bias[:, j * LANE:(j + 1) * LANE]
        qs = (q * scale).astype(BF16)
        s = _dot_t(qs, k_ref[pl.ds(d0, blk), hcols(h)].astype(BF16))
        s = jnp.where(c_i <= r_i, s, NEG)
        m = jnp.max(s, axis=-1, keepdims=True)
        p = jnp.exp(s - m)
        l = jnp.sum(p, axis=-1, keepdims=True)
        acc = _dot(p.astype(BF16), v_ref[pl.ds(d0, blk), hcols(h)].astype(BF16))
        qss.append(qs)
        init.append((m, l, acc))

    def body(t, carry):
        j = kb * t
        j0 = pl.multiple_of(j * blk, kb * blk)
        out = []
        for h in range(hp):
            m, l, acc = carry[h]
            kj = k_ref[pl.ds(j0, kb * blk), hcols(h)].astype(BF16)
            vj = v_ref[pl.ds(j0, kb * blk), hcols(h)].astype(BF16)
            bias = jnp.concatenate([bias_ref[h, j + jj] for jj in range(kb) for _ in range(blk // LANE)], axis=-1)
            s = _dot_t(qss[h], kj) + bias
            m_new = jnp.maximum(m, jnp.max(s, axis=-1, keepdims=True))
            alpha = jnp.exp(m - m_new)
            p = jnp.exp(s - m_new)
            l = alpha * l + jnp.sum(p, axis=-1, keepdims=True)
            acc = alpha * acc + _dot(p.astype(BF16), vj)
            out.append((m_new, l, acc))
        return tuple(out)

    fin = lax.fori_loop(0, (qi + kb - 1) // kb, body, tuple(init))
    o_ref[...] = jnp.concatenate([acc / l for (_, l, acc) in fin], axis=-1)


def moba_prompt_attn(z, n_seq, seq_len):
    blk = MOBA_BLOCK
    nblk = seq_len // blk
    hp, kb = MOBA_HEADS_PER_STEP, MOBA_BLOCKS_PER_ITER
    assert nblk % kb == 0 and H_A % hp == 0 and nblk < LANE
    w = hp * D_HEAD
    row = jnp.arange(LANE)[:, None]
    slab = jnp.arange(nblk * LANE)[None, :] // LANE
    spread = jnp.where(row == slab, MASK_BIG, jnp.where(row == LANE - 1, -MASK_BIG, 0.0)).astype(BF16)
    return pl.pallas_call(
        functools.partial(_moba_prompt_kernel, nblk=nblk, hp=hp, kb=kb),
        grid=(n_seq, H_A // hp, nblk),
        in_specs=[
            pl.BlockSpec((blk, w), lambda n, h, i: (n * nblk + i, CB_QA // hp + h)),
            pl.BlockSpec((seq_len, w), lambda n, h, i: (n, CB_KA // hp + h)),
            pl.BlockSpec((seq_len, w), lambda n, h, i: (n, CB_VA // hp + h)),
            pl.BlockSpec((LANE, nblk * LANE), lambda n, h, i: (0, 0)),
        ],
        out_specs=pl.BlockSpec((blk, w), lambda n, h, i: (n * nblk + i, h)),
        out_shape=jax.ShapeDtypeStruct((n_seq * seq_len, W_A), F32),
        scratch_shapes=[pltpu.VMEM((hp, nblk, D_HEAD), F32), pltpu.VMEM((hp, nblk, blk, LANE), F32)],
        compiler_params=_params("parallel", "parallel", "arbitrary"),
        name="moba_prompt",
    )(z, z, z, spread)


def _sb_block(zl, c, upper, mask):
    sp = _softplus(zl)
    lk = -sp
    if mask is not None:
        lk = jnp.where(mask, lk, 0.0)
    hi, lo = _split_bf16(lk)
    after = _dot(hi, upper) + _dot(lo, upper)
    a = jnp.exp((zl - sp) + after + c)
    if mask is not None:
        a = jnp.where(mask, a, 0.0)
    return a, c + after[:, 0:1] + lk[:, 0:1]


def _upper(n):
    r_i = lax.broadcasted_iota(jnp.int32, (n, n), 0)
    c_i = lax.broadcasted_iota(jnp.int32, (n, n), 1)
    return jnp.where(r_i > c_i, 1.0, 0.0).astype(BF16), c_i < r_i


def _sb_prompt_kernel(q_ref, k_ref, v_ref, o_ref, *, blk):
    qi = pl.program_id(2)
    scale = D_HEAD ** -0.5
    qs = (q_ref[...] * scale).astype(BF16)
    upper, causal = _upper(blk)

    def block(j, c, acc, mask):
        j0 = pl.multiple_of(j * blk, blk)
        kj = k_ref[pl.ds(j0, blk), :].astype(BF16)
        vj = v_ref[pl.ds(j0, blk), :].astype(BF16)
        a, c = _sb_block(_dot_t(qs, kj), c, upper, mask)
        return c, acc + _dot(a.astype(BF16), vj)

    c, acc = block(qi, jnp.zeros((blk, 1), F32), jnp.zeros((blk, D_HEAD), F32), causal)

    def cond(carry):
        t, c, _ = carry
        return jnp.logical_and(t < qi, jnp.max(c) > SB_LOG_FLOOR)

    def body(carry):
        t, c, acc = carry
        c, acc = block(qi - 1 - t, c, acc, None)
        return t + 1, c, acc

    _, c, acc = lax.while_loop(cond, body, (jnp.int32(0), c, acc))
    o_ref[...] = acc


def sb_prompt_attn(z, n_seq, seq_len, *, blk=256):
    nblk = seq_len // blk
    return pl.pallas_call(
        functools.partial(_sb_prompt_kernel, blk=blk),
        grid=(n_seq, H_B, nblk),
        in_specs=[
            pl.BlockSpec((blk, D_HEAD), lambda n, h, i: (n * nblk + i, CB_QB + h)),
            pl.BlockSpec((seq_len, D_HEAD), lambda n, h, i: (n, CB_KB + h)),
            pl.BlockSpec((seq_len, D_HEAD), lambda n, h, i: (n, CB_VB + h)),
        ],
        out_specs=pl.BlockSpec((blk, D_HEAD), lambda n, h, i: (n * nblk + i, h)),
        out_shape=jax.ShapeDtypeStruct((n_seq * seq_len, W_B), F32),
        compiler_params=_params("parallel", "parallel", "arbitrary"),
        name="sb_prompt",
    )(z, z, z)


def _page_head(ref, h, n_heads):
    return ref[0, 0, pl.ds(h, PAGE_SIZE, stride=n_heads), :]


def _stack_heads(x, n_heads):
    return jnp.concatenate([_head_cols(x, h) for h in range(n_heads)], axis=0)


def _unstack_heads(x, n_heads, t):
    return jnp.concatenate([x[h * t:(h + 1) * t, :] for h in range(n_heads)], axis=-1)


def _moba_sample_kernel(pt_ref, q_ref, kn_ref, vn_ref, *refs, t, n_pages):
    k_refs = refs[:n_pages]
    v_refs = refs[n_pages:2 * n_pages]
    o_ref = refs[2 * n_pages]
    s_ref, p_ref = refs[2 * n_pages + 1:]
    del pt_ref
    ppb = MOBA_BLOCK // PAGE_SIZE
    nblk = n_pages // ppb
    scale = D_HEAD ** -0.5
    rows = H_A * t
    qf = _stack_heads(q_ref[...], H_A)
    qb = (qf * scale).astype(BF16)
    r_i = lax.broadcasted_iota(jnp.int32, (rows, PAGE_SIZE * H_A), 0)
    c_i = lax.broadcasted_iota(jnp.int32, (rows, PAGE_SIZE * H_A), 1)
    head_bias = jnp.where(c_i % H_A == r_i // t, 0.0, NEG)

    ksum = []
    for i in range(n_pages):
        kp = k_refs[i][0, 0]
        ksum.append(jnp.sum(kp.reshape(PAGE_SIZE, H_A, D_HEAD), axis=0))
        s_ref[i] = _dot_t(qb, kp.astype(BF16)) + head_bias

    sc = []
    for j in range(nblk):
        mean_j = sum(ksum[j * ppb + i] for i in range(ppb)) * (1.0 / MOBA_BLOCK)
        mexp = jnp.concatenate([jnp.broadcast_to(mean_j[h:h + 1, :], (t, D_HEAD)) for h in range(H_A)], axis=0)
        sc.append(jnp.sum(qf * mexp, axis=-1, keepdims=True))
    bias = []
    for j in range(nblk):
        rank = sum(((sc[i] >= sc[j]) if i < j else (sc[i] > sc[j])).astype(jnp.int32)
                   for i in range(nblk) if i != j)
        bias.append(jnp.where(rank < MOBA_TOPK, 0.0, NEG))
    lo = _dot_t(qb, _pad_rows(_stack_heads(kn_ref[...], H_A), LANE).astype(BF16))
    r_o = lax.broadcasted_iota(jnp.int32, lo.shape, 0)
    c_o = lax.broadcasted_iota(jnp.int32, lo.shape, 1)
    lo = jnp.where((c_o // t == r_o // t) & (c_o % t <= r_o % t), lo, NEG)
    m = jnp.max(lo, axis=-1, keepdims=True)
    for pi in range(n_pages):
        m = jnp.maximum(m, jnp.max(s_ref[pi] + bias[pi // ppb], axis=-1, keepdims=True))
    po = jnp.exp(lo - m)
    l = jnp.sum(po, axis=-1, keepdims=True)
    for pi in range(n_pages):
        pp = jnp.exp(s_ref[pi] + bias[pi // ppb] - m)
        l = l + jnp.sum(pp, axis=-1, keepdims=True)
        p_ref[pi] = pp.astype(BF16)
    acc = _dot(po.astype(BF16), _pad_rows(_stack_heads(vn_ref[...], H_A), LANE).astype(BF16))
    for i in range(n_pages):
        acc = acc + _dot(p_ref[i], v_refs[i][0, 0].astype(BF16))
    o_ref[...] = _unstack_heads(acc / l, H_A, t)


def moba_sample_attn(z, pool_k, pool_v, page_table, layer, n_dec, t):
    n_pages = page_table.shape[1]
    rows = H_A * t
    zspec = lambda cb: pl.BlockSpec((t, W_A), lambda b, pt: (b, cb * LANE // W_A))
    page = lambda i: pl.BlockSpec((1, 1, PAGE_SIZE * H_A, D_HEAD), lambda b, pt: (layer, pt[b, i], 0, 0))
    grid_spec = pltpu.PrefetchScalarGridSpec(
        num_scalar_prefetch=1,
        grid=(n_dec,),
        in_specs=[zspec(CB_QA), zspec(CB_KA), zspec(CB_VA)] + [page(i) for i in range(n_pages)] * 2,
        out_specs=pl.BlockSpec((t, W_A), lambda b, pt: (b, 0)),
        scratch_shapes=[
            pltpu.VMEM((n_pages, rows, PAGE_SIZE * H_A), F32), pltpu.VMEM((n_pages, rows, PAGE_SIZE * H_A), BF16)],
    )
    return pl.pallas_call(
        functools.partial(_moba_sample_kernel, t=t, n_pages=n_pages),
        grid_spec=grid_spec,
        out_shape=jax.ShapeDtypeStruct((n_dec * t, W_A), F32),
        compiler_params=_params("parallel", vmem=2 * 2 * n_pages * PAGE_SIZE * W_A * 4 + 16 * 1024 * 1024),
        name="moba_sample",
    )(page_table, z, z, z, *([pool_k] * n_pages), *([pool_v] * n_pages))


def _sb_visitor(q_ref, acc_ref, c_ref, live_ref, t):
    scale = D_HEAD ** -0.5
    upper, _ = _upper(PAGE_SIZE)
    hrows = lambda h: slice(h * t, (h + 1) * t)
    qs = _stack_heads(q_ref[...], H_B) * scale

    def visit(k_of, v_of, mask):
        zl = jnp.concatenate([_dot_t(qs[hrows(h)].astype(BF16), k_of(h)) for h in range(H_B)], axis=0)
        a, c = _sb_block(zl, c_ref[:, 0:1], upper, mask)
        acc_ref[...] += jnp.concatenate([_dot(a[hrows(h)].astype(BF16), v_of(h)) for h in range(H_B)], axis=0)
        c_ref[...] = jnp.broadcast_to(c, c_ref.shape)
        live_ref[0] = (jnp.max(c) > SB_LOG_FLOOR).astype(jnp.int32)

    return visit


def _visit_pages(visit, k_refs, v_refs, live_ref):
    for k_ref, v_ref in zip(k_refs, v_refs):
        @pl.when(live_ref[0] > 0)
        def _(k_ref=k_ref, v_ref=v_ref):
            visit(lambda h: _page_head(k_ref, h, H_B).astype(BF16),
                  lambda h: _page_head(v_ref, h, H_B).astype(BF16), None)


def _sb_sample_head_kernel(pt_ref, q_ref, kn_ref, vn_ref, *refs, t, pg):
    k_refs, v_refs = refs[:pg], refs[pg:2 * pg]
    acc_ref, c_ref, flag_ref, live_ref = refs[2 * pg:]
    del pt_ref
    visit = _sb_visitor(q_ref, acc_ref, c_ref, live_ref, t)
    acc_ref[...] = jnp.zeros_like(acc_ref)
    c_ref[...] = jnp.zeros_like(c_ref)
    kn, vn = kn_ref[...], vn_ref[...]
    r_t = lax.broadcasted_iota(jnp.int32, (H_B * t, PAGE_SIZE), 0) % t
    c_j = lax.broadcasted_iota(jnp.int32, (H_B * t, PAGE_SIZE), 1)
    visit(lambda h: _pad_rows(_head_cols(kn, h), PAGE_SIZE).astype(BF16),
          lambda h: _pad_rows(_head_cols(vn, h), PAGE_SIZE).astype(BF16), c_j < r_t)
    _visit_pages(visit, k_refs, v_refs, live_ref)
    flag_ref[...] = jnp.full(flag_ref.shape, live_ref[0], jnp.int32)


def _sb_sample_tail_kernel(pt_ref, flag_s_ref, q_ref, acc_in_ref, c_in_ref, *refs, t, pg):
    k_refs, v_refs = refs[:pg], refs[pg:2 * pg]
    o_ref, acc_ref, c_ref, live_ref = refs[2 * pg:]
    del pt_ref
    s = pl.program_id(1)

    @pl.when(s == 0)
    def _():
        acc_ref[...] = acc_in_ref[...]
        c_ref[...] = c_in_ref[...]
        live_ref[0] = flag_s_ref[pl.program_id(0)]

    @pl.when(live_ref[0] > 0)
    def _():
        visit = _sb_visitor(q_ref, acc_ref, c_ref, live_ref, t)
        _visit_pages(visit, k_refs, v_refs, live_ref)

    @pl.when(s == pl.num_programs(1) - 1)
    def _():
        o_ref[...] = _unstack_heads(acc_ref[...], H_B, t)


def sb_sample_attn(z, pool_k, pool_v, page_table, layer, n_dec, t, *, head_pages=2, tail_pg=14):
    n_pages = page_table.shape[1]
    head_pages = min(head_pages, n_pages)
    rows = H_B * t
    page_block = (1, 1, PAGE_SIZE * H_B, D_HEAD)
    acc_spec = lambda *_: pl.BlockSpec((rows, LANE), lambda b, *_: (b, 0))
    head_page = lambda i: pl.BlockSpec(page_block, lambda b, pt: (layer, pt[b, n_pages - 1 - i], 0, 0))
    acc, c, flag = pl.pallas_call(
        functools.partial(_sb_sample_head_kernel, t=t, pg=head_pages),
        grid_spec=pltpu.PrefetchScalarGridSpec(
            num_scalar_prefetch=1,
            grid=(n_dec,),
            in_specs=[pl.BlockSpec((t, W_B), lambda b, pt, cb=cb: (b, cb * LANE // W_B)) for cb in (CB_QB, CB_KB, CB_VB)]
            + [head_page(i) for i in range(head_pages)] * 2,
            out_specs=[acc_spec(), acc_spec(), pl.BlockSpec((8, LANE), lambda b, pt: (b, 0))],
            scratch_shapes=[pltpu.SMEM((1,), jnp.int32)],
        ),
        out_shape=[jax.ShapeDtypeStruct((n_dec * rows, LANE), F32), jax.ShapeDtypeStruct((n_dec * rows, LANE), F32),
                   jax.ShapeDtypeStruct((n_dec * 8, LANE), jnp.int32)],
        compiler_params=_params("parallel"),
        name="sb_sample_head",
    )(page_table, z, z, z, *([pool_k] * head_pages), *([pool_v] * head_pages))
    n_tail = n_pages - head_pages
    assert n_tail > 0
    tail_pg = math.gcd(n_tail, tail_pg)
    steps = n_tail // tail_pg
    live = flag[::8, 0]

    def tail_page(i):
        def index(b, s, pt, lv):
            pos = n_pages - 1 - (head_pages + s * tail_pg + i)
            return (layer, jnp.where(lv[b] > 0, pt[b, pos], 0), 0, 0)
        return pl.BlockSpec(page_block, index)

    return pl.pallas_call(
        functools.partial(_sb_sample_tail_kernel, t=t, pg=tail_pg),
        grid_spec=pltpu.PrefetchScalarGridSpec(
            num_scalar_prefetch=2,
            grid=(n_dec, steps),
            in_specs=[pl.BlockSpec((t, W_B), lambda b, s, pt, lv: (b, CB_QB * LANE // W_B)),
                      pl.BlockSpec((rows, LANE), lambda b, s, pt, lv: (b, 0)),
                      pl.BlockSpec((rows, LANE), lambda b, s, pt, lv: (b, 0))]
            + [tail_page(i) for i in range(tail_pg)] * 2,
            out_specs=pl.BlockSpec((t, W_B), lambda b, s, pt, lv: (b, 0)),
            scratch_shapes=[pltpu.VMEM((rows, LANE), F32), pltpu.VMEM((rows, LANE), F32), pltpu.SMEM((1,), jnp.int32)],
        ),
        out_shape=jax.ShapeDtypeStruct((n_dec * t, W_B), F32),
        compiler_params=_params("parallel", "arbitrary"),
        name="sb_sample_tail",
    )(page_table, live, z, acc, c, *([pool_k] * tail_pg), *([pool_v] * tail_pg))


def _gla_kernel(q_ref, k_ref, v_ref, rc_ref, w2_ref, b_ref, s0_ref, o_ref, sfin_ref, st_ref, *, c, nchunk):
    step = pl.program_id(1)

    @pl.when(step == 0)
    def _():
        st_ref[...] = jnp.zeros_like(st_ref)
        for h in range(H_C):
            st_ref[h * DK_C:(h + 1) * DK_C, h * DV_C:(h + 1) * DV_C] = s0_ref[0, h]

    r_i = lax.broadcasted_iota(jnp.int32, (c, c), 0)
    c_i = lax.broadcasted_iota(jnp.int32, (c, c), 1)
    tril = jnp.where(c_i <= r_i, 1.0, 0.0).astype(F32)
    s_idx = lax.broadcasted_iota(jnp.int32, (c, WK_C), 0)
    kh = lax.broadcasted_iota(jnp.int32, (WK_C, WV_C), 0) // DK_C
    vh = lax.broadcasted_iota(jnp.int32, (WK_C, WV_C), 1) // DV_C
    diag = kh == vh
    expand = jnp.where(diag, 1.0, 0.0).astype(BF16)
    pad_row = lax.broadcasted_iota(jnp.int32, (LANE, WK_C), 0)
    pad_col = lax.broadcasted_iota(jnp.int32, (WK_C, LANE), 1)

    def chunk(ci, carry):
        r0 = pl.multiple_of(ci * c, c)
        q = q_ref[pl.ds(r0, c), :] * (DK_C ** -0.5)
        k = k_ref[pl.ds(r0, c), :]
        v = v_ref[pl.ds(r0, c), :]
        x = _dot(rc_ref[pl.ds(r0, c), :].astype(BF16), w2_ref[...]) + b_ref[...]
        la = (jnp.minimum(x, 0.0) - jnp.log(1.0 + jnp.exp(-jnp.abs(x)))) * (1.0 / GLA_GATE_NORM)
        bc = _dot(tril, la, precision=HIGHEST)
        ps = []
        for t in range(c):
            diff = bc[t:t + 1, :] - bc
            dec = jnp.exp(jnp.where(s_idx <= t, diff, -jnp.inf))
            ps.append((q[t:t + 1, :] * dec) * k)
        a = _dot(jnp.concatenate(ps, axis=0).astype(BF16), expand)
        o_rows = [jnp.sum(a[t * c:(t + 1) * c, :] * v, axis=0, keepdims=True) for t in range(c)]
        st = st_ref[...]
        o = jnp.concatenate(o_rows, axis=0) + _dot((q * jnp.exp(bc)).astype(BF16), st.astype(BF16))
        o_ref[pl.ds(r0, c), :] = o
        b_last = bc[c - 1:c, :]
        kd = k * jnp.exp(b_last - bc)
        packed = jnp.where(pad_row == c, jnp.broadcast_to(b_last, (LANE, WK_C)), _pad_rows(kd, LANE))
        packed_t = packed.T
        g_col = jnp.exp(packed_t[:, c:c + 1])
        kd_t = jnp.where(pad_col < c, packed_t, 0.0).astype(BF16)
        ds = _dot(kd_t, _pad_rows(v, LANE).astype(BF16))
        st_ref[...] = st * g_col + jnp.where(diag, ds, 0.0)
        return carry

    lax.fori_loop(0, nchunk, chunk, 0, unroll=min(2, nchunk))

    @pl.when(step == pl.num_programs(1) - 1)
    def _():
        for h in range(H_C):
            sfin_ref[0, h] = st_ref[h * DK_C:(h + 1) * DK_C, h * DV_C:(h + 1) * DV_C]


def gla(z, w2, b, s0, n, length, *, tb=256):
    c = math.gcd(length, GLA_CHUNK)
    tb = math.gcd(length, tb)
    steps = length // tb
    w2p = jnp.concatenate([w2, jnp.zeros((LANE - GLA_RANK, WK_C), w2.dtype)], axis=0).astype(BF16)
    row = lambda i, s: i * steps + s
    return pl.pallas_call(
        functools.partial(_gla_kernel, c=c, nchunk=tb // c),
        grid=(n, steps),
        in_specs=[
            pl.BlockSpec((tb, WK_C), lambda i, s: (row(i, s), CB_QC * LANE // WK_C)),
            pl.BlockSpec((tb, WK_C), lambda i, s: (row(i, s), CB_KC * LANE // WK_C)),
            pl.BlockSpec((tb, WV_C), lambda i, s: (row(i, s), CB_VC * LANE // WV_C)),
            pl.BlockSpec((tb, LANE), lambda i, s: (row(i, s), CB_RC)),
            pl.BlockSpec((LANE, WK_C), lambda i, s: (0, 0)),
            pl.BlockSpec((1, WK_C), lambda i, s: (0, 0)),
            pl.BlockSpec((1, H_C, DK_C, DV_C), lambda i, s: (i, 0, 0, 0)),
        ],
        out_specs=[
            pl.BlockSpec((tb, WV_C), lambda i, s: (row(i, s), 0)),
            pl.BlockSpec((1, H_C, DK_C, DV_C), lambda i, s: (i, 0, 0, 0)),
        ],
        out_shape=[jax.ShapeDtypeStruct((n * length, WV_C), F32), jax.ShapeDtypeStruct((n, H_C, DK_C, DV_C), F32)],
        scratch_shapes=[pltpu.VMEM((WK_C, WV_C), F32)],
        compiler_params=_params("parallel", "arbitrary"),
        name="gla",
    )(z, z, z, z, w2p, b.reshape(1, WK_C), s0)


def _branch_kernel(ma_ref, mb_ref, mc_ref, ga_ref, gb_ref, gc_ref, oa_ref, ob_ref, oc_ref, gain_ref,
                   wa_ref, wb_ref, wc_ref, m_ref):
    silu = lambda g: g * _sigmoid(g)
    ya = _dot((oa_ref[...] * silu(ga_ref[...])).astype(BF16), wa_ref[...])
    yb = _dot((ob_ref[...] * silu(gb_ref[...])).astype(BF16), wb_ref[...])
    oc = oc_ref[...]
    gain = gain_ref[...]
    heads = []
    for h in range(H_C):
        oh = oc[:, h * DV_C:(h + 1) * DV_C]
        heads.append(oh * lax.rsqrt(jnp.mean(oh * oh, axis=-1, keepdims=True) + RMS_EPS) * gain)
    ocn = jnp.concatenate(heads, axis=-1)
    yc = _dot((ocn * silu(gc_ref[...])).astype(BF16), wc_ref[...])
    m = _sigmoid(ma_ref[...]) * ya + _sigmoid(mb_ref[...]) * yb + _sigmoid(mc_ref[...]) * yc
    m_ref[...] = m.astype(BF16)


def branch_merge(z, oa, ob, oc, gla_gain, wa, wb, wc, layer, *, tm=256):
    m = z.shape[0]
    tm = math.gcd(m, tm)
    zb = lambda cb, w: pl.BlockSpec((tm, w), lambda i: (i, cb * LANE // w))
    rows = lambda w: pl.BlockSpec((tm, w), lambda i: (i, 0))
    wspec = lambda w: pl.BlockSpec((None, w, D_MODEL), lambda i: (layer, 0, 0))
    return pl.pallas_call(
        _branch_kernel,
        grid=(m // tm,),
        in_specs=[zb(CB_MA, D_MODEL), zb(CB_MB, D_MODEL), zb(CB_MC, D_MODEL),
                  zb(CB_GA, W_A), zb(CB_GB, W_B), zb(CB_GC, WV_C),
                  rows(W_A), rows(W_B), rows(WV_C), pl.BlockSpec((1, DV_C), lambda i: (0, 0)),
                  wspec(W_A), wspec(W_B), wspec(WV_C)],
        out_specs=rows(D_MODEL),
        out_shape=jax.ShapeDtypeStruct((m, D_MODEL), BF16),
        compiler_params=_params("parallel"),
        name="branch_merge",
    )(z, z, z, z, z, z, oa, ob, oc, gla_gain.reshape(1, DV_C), wa, wb, wc)


def _outproj_kernel(x_ref, m_ref, w_ref, g_ref, o_ref, *, final):
    y = x_ref[...] + _dot(m_ref[...], w_ref[...])
    if final:
        y = y * lax.rsqrt(jnp.mean(y * y, axis=-1, keepdims=True) + RMS_EPS) * g_ref[...]
    o_ref[...] = y


def out_project(x, m, w_out, final_gain, layer, *, final, tm=512):
    rows, d = x.shape
    tm = math.gcd(rows, tm)
    return pl.pallas_call(
        functools.partial(_outproj_kernel, final=final),
        grid=(rows // tm,),
        in_specs=[pl.BlockSpec((tm, d), lambda i: (i, 0)), pl.BlockSpec((tm, d), lambda i: (i, 0)),
                  pl.BlockSpec((None, d, d), lambda i: (layer, 0, 0)), pl.BlockSpec((1, d), lambda i: (0, 0))],
        out_specs=pl.BlockSpec((tm, d), lambda i: (i, 0)),
        out_shape=jax.ShapeDtypeStruct((rows, d), F32),
        compiler_params=_params("parallel"),
        name="out_project",
    )(x, m, w_out, final_gain.reshape(1, d))


def _kv_writer_kernel(*refs, heads, n_layers):
    nseg = len(heads)
    ins, outs = refs[:n_layers * nseg], refs[n_layers * nseg:]
    layer = pl.program_id(0)
    for li in range(n_layers):
        @pl.when(layer == li)
        def _(li=li):
            for sg, nh in enumerate(heads):
                src, dst = ins[li * nseg + sg], outs[sg]
                for h in range(nh):
                    dst[0, pl.ds(h, src.shape[0], stride=nh), :] = src[:, h * D_HEAD:(h + 1) * D_HEAD]


def kv_rows(zs, *, tm=512):
    n_layers = len(zs)
    rows = zs[0].shape[0]
    tm = math.gcd(rows, tm)
    nt = rows // tm
    segs = ((CB_KA, H_A), (CB_VA, H_A), (CB_KB, H_B), (CB_VB, H_B))
    heads = tuple(nh for _, nh in segs)

    def in_spec(li, cb, nh):
        w = nh * D_HEAD
        return pl.BlockSpec((tm, w), lambda l, i: (jnp.clip(i + (l - li) * nt, 0, nt - 1), cb * LANE // w))

    return pl.pallas_call(
        functools.partial(_kv_writer_kernel, heads=heads, n_layers=n_layers),
        grid=(n_layers, nt),
        in_specs=[in_spec(li, cb, nh) for li in range(n_layers) for cb, nh in segs],
        out_specs=[pl.BlockSpec((1, tm * nh, D_HEAD), lambda l, i: (l, i, 0)) for nh in heads],
        out_shape=[jax.ShapeDtypeStruct((n_layers, rows * nh, D_HEAD), F32) for nh in heads],
        compiler_params=_params("arbitrary", "arbitrary"),
        name="kv_rows",
    )(*[z for z in zs for _ in segs])


def kernel(x_prompt, x_sample, cache_moba_k, cache_moba_v, cache_sb_k, cache_sb_v, state_gla, page_table,
           norm_gain, w_in, gla_w2, gla_b, gla_out_gain, w_br_a, w_br_b, w_br_c, w_out, final_gain):
    n_seq, seq_len, _ = x_prompt.shape
    n_dec, dec_len, _ = x_sample.shape
    n_pool = cache_moba_k.shape[1]
    x_p = x_prompt.reshape(n_seq * seq_len, D_MODEL)
    x_s = x_sample.reshape(n_dec * dec_len, D_MODEL)
    w_all = jnp.concatenate(
        [w_in[:, :, MAIN_W + GLA_RANK:], w_in[:, :, :MAIN_W], w_in[:, :, MAIN_W:MAIN_W + GLA_RANK],
         jnp.zeros((DEPTH, D_MODEL, RC_PAD - GLA_RANK), w_in.dtype)], axis=-1).astype(BF16)
    wa, wb, wc, wo = (w.astype(BF16) for w in (w_br_a, w_br_b, w_br_c, w_out))
    pool_ak = cache_moba_k.reshape(DEPTH, n_pool, PAGE_SIZE * H_A, D_HEAD)
    pool_av = cache_moba_v.reshape(DEPTH, n_pool, PAGE_SIZE * H_A, D_HEAD)
    pool_bk = cache_sb_k.reshape(DEPTH, n_pool, PAGE_SIZE * H_B, D_HEAD)
    pool_bv = cache_sb_v.reshape(DEPTH, n_pool, PAGE_SIZE * H_B, D_HEAD)
    zero_state = jnp.zeros((n_seq, H_C, DK_C, DV_C), F32)

    zs_p, zs_s, st_p, st_s = [], [], [], []
    for l in range(DEPTH):
        last = l == DEPTH - 1
        z = norm_project(x_p, norm_gain[l], w_all, l)
        oa = moba_prompt_attn(z, n_seq, seq_len)
        ob = sb_prompt_attn(z, n_seq, seq_len)
        oc, sf = gla(z, gla_w2[l], gla_b[l], zero_state, n_seq, seq_len)
        m = branch_merge(z, oa, ob, oc, gla_out_gain[l], wa, wb, wc, l)
        x_p = out_project(x_p, m, wo, final_gain, l, final=last)
        zs_p.append(z)
        st_p.append(sf)
        z = norm_project(x_s, norm_gain[l], w_all, l)
        oa = moba_sample_attn(z, pool_ak, pool_av, page_table, l, n_dec, dec_len)
        ob = sb_sample_attn(z, pool_bk, pool_bv, page_table, l, n_dec, dec_len)
        oc, sf = gla(z, gla_w2[l], gla_b[l], state_gla[l].astype(F32), n_dec, dec_len)
        m = branch_merge(z, oa, ob, oc, gla_out_gain[l], wa, wb, wc, l)
        x_s = out_project(x_s, m, wo, final_gain, l, final=last)
        zs_s.append(z)
        st_s.append(sf)

    kv_p = kv_rows(zs_p)
    kv_s = kv_rows(zs_s)
    shape_p = lambda a, nh: a.reshape(DEPTH, n_seq, seq_len, nh, D_HEAD)
    shape_s = lambda a, nh: a.reshape(DEPTH, n_dec, dec_len, nh, D_HEAD)
    return (x_p.reshape(n_seq, seq_len, D_MODEL), x_s.reshape(n_dec, dec_len, D_MODEL),
            shape_p(kv_p[0], H_A), shape_p(kv_p[1], H_A), shape_p(kv_p[2], H_B), shape_p(kv_p[3], H_B),
            jnp.stack(st_p),
            shape_s(kv_s[0], H_A), shape_s(kv_s[1], H_A), shape_s(kv_s[2], H_B), shape_s(kv_s[3], H_B),
            jnp.stack(st_s))
```

```python
import functools
import math

import jax
import jax.numpy as jnp
from jax import lax
from jax.experimental import pallas as pl
from jax.experimental.pallas import tpu as pltpu

D_MODEL = 2048
DEPTH = 2
PAGE_SIZE = 128
D_HEAD = 128
H_A = 8
H_B = 4
H_C = 4
DK_C = 64
DV_C = 128
W_A = H_A * D_HEAD
W_B = H_B * D_HEAD
WK_C = H_C * DK_C
WV_C = H_C * DV_C
GLA_RANK = 16
GLA_GATE_NORM = 16.0
GLA_CHUNK = 16
MOBA_BLOCK = 256
MOBA_TOPK = 3
RMS_EPS = 1e-6

LANE = 128
MAIN_W = 4 * W_A + 4 * W_B + 2 * WK_C + 2 * WV_C
GATE_W = 3 * D_MODEL
RC_PAD = 512
Z_W = GATE_W + MAIN_W + RC_PAD
CB_MA, CB_MB, CB_MC = 0, 16, 32
CB_QA, CB_KA, CB_VA, CB_GA = 48, 56, 64, 72
CB_QB, CB_KB, CB_VB, CB_GB = 80, 84, 88, 92
CB_QC, CB_KC, CB_VC, CB_GC = 96, 98, 100, 104
CB_RC = 108

NEG = -1e30
SB_LOG_FLOOR = -104.0
VMEM_LIMIT = 48 * 1024 * 1024
MOBA_HEADS_PER_STEP = 2
MOBA_BLOCKS_PER_ITER = 4
MASK_BIG = 2.0 ** 100

F32 = jnp.float32
BF16 = jnp.bfloat16
HIGHEST = lax.Precision.HIGHEST


def _dot_t(a, b, **kw):
    return lax.dot_general(a, b, (((1,), (1,)), ((), ())), preferred_element_type=F32, **kw)


def _dot(a, b, **kw):
    return jnp.dot(a, b, preferred_element_type=F32, **kw)


def _softplus(z):
    return jnp.maximum(z, 0.0) + jnp.log(1.0 + jnp.exp(-jnp.abs(z)))


def _sigmoid(z):
    return 1.0 / (1.0 + jnp.exp(-z))


def _split_bf16(x):
    hi = x.astype(BF16)
    lo = (x - hi.astype(F32)).astype(BF16)
    return hi, lo


def _params(*sem, vmem=VMEM_LIMIT):
    return pltpu.CompilerParams(dimension_semantics=sem, vmem_limit_bytes=vmem)


def _pad_rows(x, rows):
    return jnp.concatenate([x, jnp.zeros((rows - x.shape[0], x.shape[1]), x.dtype)], axis=0)


def _head_cols(x, h):
    return x[:, h * D_HEAD:(h + 1) * D_HEAD]


def _proj_kernel(x_ref, g_ref, w_ref, o_ref, xn_ref):
    @pl.when(pl.program_id(1) == 0)
    def _():
        x = x_ref[...]
        ms = jnp.mean(x * x, axis=-1, keepdims=True)
        xn_ref[...] = (x * lax.rsqrt(ms + RMS_EPS) * g_ref[...]).astype(BF16)

    o_ref[...] = _dot(xn_ref[...], w_ref[...])


def norm_project(x, gain, w_bf16, layer, *, tm=1024, tn=1024):
    m, d = x.shape
    n = w_bf16.shape[2]
    tm = math.gcd(m, tm)
    return pl.pallas_call(
        _proj_kernel,
        grid=(m // tm, n // tn),
        in_specs=[
            pl.BlockSpec((tm, d), lambda i, j: (i, 0)),
            pl.BlockSpec((1, d), lambda i, j: (0, 0)),
            pl.BlockSpec((None, d, tn), lambda i, j: (layer, 0, j)),
        ],
        out_specs=pl.BlockSpec((tm, tn), lambda i, j: (i, j)),
        out_shape=jax.ShapeDtypeStruct((m, n), F32),
        scratch_shapes=[pltpu.VMEM((tm, d), BF16)],
        compiler_params=_params("parallel", "arbitrary"),
        name="norm_project",
    )(x, gain.reshape(1, d), w_bf16)


def _topk_select_t(sc_t, allowed, nblk):
    blk_i = lax.broadcasted_iota(jnp.int32, sc_t.shape, 0)
    sc_t = jnp.where(allowed, sc_t, -jnp.inf)
    rank = jnp.zeros(sc_t.shape, jnp.int32)
    for i in range(nblk):
        si = sc_t[i:i + 1, :]
        ge = (si >= sc_t).astype(jnp.int32)
        gt = (si > sc_t).astype(jnp.int32)
        rank = rank + jnp.where(blk_i > i, ge, gt)
    return jnp.where(allowed, (rank < MOBA_TOPK).astype(F32), 0.0)


def _moba_prompt_kernel(q_ref, k_ref, v_ref, spread_ref, o_ref, means_ref, bias_ref, *, nblk, hp, kb):
    qi = pl.program_id(2)
    blk = MOBA_BLOCK
    scale = D_HEAD ** -0.5
    hcols = lambda h: slice(h * D_HEAD, (h + 1) * D_HEAD)

    @pl.when(qi == 0)
    def _():
        for h in range(hp):
            for j in range(nblk):
                means_ref[h, j:j + 1, :] = (
                    jnp.sum(k_ref[j * blk:(j + 1) * blk, hcols(h)], axis=0, keepdims=True) * (1.0 / blk))

    d0 = pl.multiple_of(qi * blk, blk)
    r_i = lax.broadcasted_iota(jnp.int32, (blk, blk), 0)
    c_i = lax.broadcasted_iota(jnp.int32, (blk, blk), 1)
    qss, init = [], []
    for h in range(hp):
        q = q_ref[:, hcols(h)]
        sc_t = _dot_t(means_ref[h], q, precision=HIGHEST)
        blk_i = lax.broadcasted_iota(jnp.int32, sc_t.shape, 0)
        sel_t = _topk_select_t(sc_t, blk_i < qi, nblk)
        pad = jnp.where(lax.broadcasted_iota(jnp.int32, (LANE - nblk, blk), 0) == LANE - nblk - 1, 1.0, 0.0)
        sel = jnp.concatenate([sel_t, pad], axis=0).T.astype(BF16)
        bias = _dot(sel, spread_ref[...])
        for j in range(nblk):
            bias_ref[h, j] = bias[:, j * LANE:(j + 1) * LANE]
        qs = (q * scale).astype(BF16)
        s = _dot_t(qs, k_ref[pl.ds(d0, blk), hcols(h)].astype(BF16))
        s = jnp.where(c_i <= r_i, s, NEG)
        m = jnp.max(s, axis=-1, keepdims=True)
        p = jnp.exp(s - m)
        l = jnp.sum(p, axis=-1, keepdims=True)
        acc = _dot(p.astype(BF16), v_ref[pl.ds(d0, blk), hcols(h)].astype(BF16))
        qss.append(qs)
        init.append((m, l, acc))

    def body(t, carry):
        j = kb * t
        j0 = pl.multiple_of(j * blk, kb * blk)
        out = []
        for h in range(hp):
            m, l, acc = carry[h]
            kj = k_ref[pl.ds(j0, kb * blk), hcols(h)].astype(BF16)
            vj = v_ref[pl.ds(j0, kb * blk), hcols(h)].astype(BF16)
            bias = jnp.concatenate([bias_ref[h, j + jj] for jj in range(kb) for _ in range(blk // LANE)], axis=-1)
            s = _dot_t(qss[h], kj) + bias
            m_new = jnp.maximum(m, jnp.max(s, axis=-1, keepdims=True))
            alpha = jnp.exp(m - m_new)
            p = jnp.exp(s - m_new)
            l = alpha * l + jnp.sum(p, axis=-1, keepdims=True)
            acc = alpha * acc + _dot(p.astype(BF16), vj)
            out.append((m_new, l, acc))
        return tuple(out)

    fin = lax.fori_loop(0, (qi + kb - 1) // kb, body, tuple(init))
    o_ref[...] = jnp.concatenate([acc / l for (_, l, acc) in fin], axis=-1)


def moba_prompt_attn(z, n_seq, seq_len):
    blk = MOBA_BLOCK
    nblk = seq_len // blk
    hp, kb = MOBA_HEADS_PER_STEP, MOBA_BLOCKS_PER_ITER
    assert nblk % kb == 0 and H_A % hp == 0 and nblk < LANE
    w = hp * D_HEAD
    row = jnp.arange(LANE)[:, None]
    slab = jnp.arange(nblk * LANE)[None, :] // LANE
    spread = jnp.where(row == slab, MASK_BIG, jnp.where(row == LANE - 1, -MASK_BIG, 0.0)).astype(BF16)
    return pl.pallas_call(
        functools.partial(_moba_prompt_kernel, nblk=nblk, hp=hp, kb=kb),
        grid=(n_seq, H_A // hp, nblk),
        in_specs=[
            pl.BlockSpec((blk, w), lambda n, h, i: (n * nblk + i, CB_QA // hp + h)),
            pl.BlockSpec((seq_len, w), lambda n, h, i: (n, CB_KA // hp + h)),
            pl.BlockSpec((seq_len, w), lambda n, h, i: (n, CB_VA // hp + h)),
            pl.BlockSpec((LANE, nblk * LANE), lambda n, h, i: (0, 0)),
        ],
        out_specs=pl.BlockSpec((blk, w), lambda n, h, i: (n * nblk + i, h)),
        out_shape=jax.ShapeDtypeStruct((n_seq * seq_len, W_A), F32),
        scratch_shapes=[pltpu.VMEM((hp, nblk, D_HEAD), F32), pltpu.VMEM((hp, nblk, blk, LANE), F32)],
        compiler_params=_params("parallel", "parallel", "arbitrary"),
        name="moba_prompt",
    )(z, z, z, spread)


def _sb_block(zl, c, upper, mask):
    sp = _softplus(zl)
    lk = -sp
    if mask is not None:
        lk = jnp.where(mask, lk, 0.0)
    hi, lo = _split_bf16(lk)
    after = _dot(hi, upper) + _dot(lo, upper)
    a = jnp.exp((zl - sp) + after + c)
    if mask is not None:
        a = jnp.where(mask, a, 0.0)
    return a, c + after[:, 0:1] + lk[:, 0:1]


def _upper(n):
    r_i = lax.broadcasted_iota(jnp.int32, (n, n), 0)
    c_i = lax.broadcasted_iota(jnp.int32, (n, n), 1)
    return jnp.where(r_i > c_i, 1.0, 0.0).astype(BF16), c_i < r_i


def _sb_prompt_kernel(q_ref, k_ref, v_ref, o_ref, *, blk):
    qi = pl.program_id(2)
    scale = D_HEAD ** -0.5
    qs = (q_ref[...] * scale).astype(BF16)
    upper, causal = _upper(blk)

    def block(j, c, acc, mask):
        j0 = pl.multiple_of(j * blk, blk)
        kj = k_ref[pl.ds(j0, blk), :].astype(BF16)
        vj = v_ref[pl.ds(j0, blk), :].astype(BF16)
        a, c = _sb_block(_dot_t(qs, kj), c, upper, mask)
        return c, acc + _dot(a.astype(BF16), vj)

    c, acc = block(qi, jnp.zeros((blk, 1), F32), jnp.zeros((blk, D_HEAD), F32), causal)

    def cond(carry):
        t, c, _ = carry
        return jnp.logical_and(t < qi, jnp.max(c) > SB_LOG_FLOOR)

    def body(carry):
        t, c, acc = carry
        c, acc = block(qi - 1 - t, c, acc, None)
        return t + 1, c, acc

    _, c, acc = lax.while_loop(cond, body, (jnp.int32(0), c, acc))
    o_ref[...] = acc


def sb_prompt_attn(z, n_seq, seq_len, *, blk=256):
    nblk = seq_len // blk
    return pl.pallas_call(
        functools.partial(_sb_prompt_kernel, blk=blk),
        grid=(n_seq, H_B, nblk),
        in_specs=[
            pl.BlockSpec((blk, D_HEAD), lambda n, h, i: (n * nblk + i, CB_QB + h)),
            pl.BlockSpec((seq_len, D_HEAD), lambda n, h, i: (n, CB_KB + h)),
            pl.BlockSpec((seq_len, D_HEAD), lambda n, h, i: (n, CB_VB + h)),
        ],
        out_specs=pl.BlockSpec((blk, D_HEAD), lambda n, h, i: (n * nblk + i, h)),
        out_shape=jax.ShapeDtypeStruct((n_seq * seq_len, W_B), F32),
        compiler_params=_params("parallel", "parallel", "arbitrary"),
        name="sb_prompt",
    )(z, z, z)


def _page_head(ref, h, n_heads):
    return ref[0, 0, pl.ds(h, PAGE_SIZE, stride=n_heads), :]


def _stack_heads(x, n_heads):
    return jnp.concatenate([_head_cols(x, h) for h in range(n_heads)], axis=0)


def _unstack_heads(x, n_heads, t):
    return jnp.concatenate([x[h * t:(h + 1) * t, :] for h in range(n_heads)], axis=-1)


def _moba_sample_kernel(pt_ref, q_ref, kn_ref, vn_ref, *refs, t, n_pages):
    k_refs = refs[:n_pages]
    v_refs = refs[n_pages:2 * n_pages]
    o_ref = refs[2 * n_pages]
    s_ref, p_ref = refs[2 * n_pages + 1:]
    del pt_ref
    ppb = MOBA_BLOCK // PAGE_SIZE
    nblk = n_pages // ppb
    scale = D_HEAD ** -0.5
    rows = H_A * t
    qf = _stack_heads(q_ref[...], H_A)
    qb = (qf * scale).astype(BF16)
    r_i = lax.broadcasted_iota(jnp.int32, (rows, PAGE_SIZE * H_A), 0)
    c_i = lax.broadcasted_iota(jnp.int32, (rows, PAGE_SIZE * H_A), 1)
    head_bias = jnp.where(c_i % H_A == r_i // t, 0.0, NEG)

    ksum = []
    for i in range(n_pages):
        kp = k_refs[i][0, 0]
        ksum.append(jnp.sum(kp.reshape(PAGE_SIZE, H_A, D_HEAD), axis=0))
        s_ref[i] = _dot_t(qb, kp.astype(BF16)) + head_bias

    sc = []
    for j in range(nblk):
        mean_j = sum(ksum[j * ppb + i] for i in range(ppb)) * (1.0 / MOBA_BLOCK)
        mexp = jnp.concatenate([jnp.broadcast_to(mean_j[h:h + 1, :], (t, D_HEAD)) for h in range(H_A)], axis=0)
        sc.append(jnp.sum(qf * mexp, axis=-1, keepdims=True))
    bias = []
    for j in range(nblk):
        rank = sum(((sc[i] >= sc[j]) if i < j else (sc[i] > sc[j])).astype(jnp.int32)
                   for i in range(nblk) if i != j)
        bias.append(jnp.where(rank < MOBA_TOPK, 0.0, NEG))
    lo = _dot_t(qb, _pad_rows(_stack_heads(kn_ref[...], H_A), LANE).astype(BF16))
    r_o = lax.broadcasted_iota(jnp.int32, lo.shape, 0)
    c_o = lax.broadcasted_iota(jnp.int32, lo.shape, 1)
    lo = jnp.where((c_o // t == r_o // t) & (c_o % t <= r_o % t), lo, NEG)
    m = jnp.max(lo, axis=-1, keepdims=True)
    for pi in range(n_pages):
        m = jnp.maximum(m, jnp.max(s_ref[pi] + bias[pi // ppb], axis=-1, keepdims=True))
    po = jnp.exp(lo - m)
    l = jnp.sum(po, axis=-1, keepdims=True)
    for pi in range(n_pages):
        pp = jnp.exp(s_ref[pi] + bias[pi // ppb] - m)
        l = l + jnp.sum(pp, axis=-1, keepdims=True)
        p_ref[pi] = pp.astype(BF16)
    acc = _dot(po.astype(BF16), _pad_rows(_stack_heads(vn_ref[...], H_A), LANE).astype(BF16))
    for i in range(n_pages):
        acc = acc + _dot(p_ref[i], v_refs[i][0, 0].astype(BF16))
    o_ref[...] = _unstack_heads(acc / l, H_A, t)


def moba_sample_attn(z, pool_k, pool_v, page_table, layer, n_dec, t):
    n_pages = page_table.shape[1]
    rows = H_A * t
    zspec = lambda cb: pl.BlockSpec((t, W_A), lambda b, pt: (b, cb * LANE // W_A))
    page = lambda i: pl.BlockSpec((1, 1, PAGE_SIZE * H_A, D_HEAD), lambda b, pt: (layer, pt[b, i], 0, 0))
    grid_spec = pltpu.PrefetchScalarGridSpec(
        num_scalar_prefetch=1,
        grid=(n_dec,),
        in_specs=[zspec(CB_QA), zspec(CB_KA), zspec(CB_VA)] + [page(i) for i in range(n_pages)] * 2,
        out_specs=pl.BlockSpec((t, W_A), lambda b, pt: (b, 0)),
        scratch_shapes=[
            pltpu.VMEM((n_pages, rows, PAGE_SIZE * H_A), F32), pltpu.VMEM((n_pages, rows, PAGE_SIZE * H_A), BF16)],
    )
    return pl.pallas_call(
        functools.partial(_moba_sample_kernel, t=t, n_pages=n_pages),
        grid_spec=grid_spec,
        out_shape=jax.ShapeDtypeStruct((n_dec * t, W_A), F32),
        compiler_params=_params("parallel", vmem=2 * 2 * n_pages * PAGE_SIZE * W_A * 4 + 16 * 1024 * 1024),
        name="moba_sample",
    )(page_table, z, z, z, *([pool_k] * n_pages), *([pool_v] * n_pages))


def _sb_visitor(q_ref, acc_ref, c_ref, live_ref, t):
    scale = D_HEAD ** -0.5
    upper, _ = _upper(PAGE_SIZE)
    hrows = lambda h: slice(h * t, (h + 1) * t)
    qs = _stack_heads(q_ref[...], H_B) * scale

    def visit(k_of, v_of, mask):
        zl = jnp.concatenate([_dot_t(qs[hrows(h)].astype(BF16), k_of(h)) for h in range(H_B)], axis=0)
        a, c = _sb_block(zl, c_ref[:, 0:1], upper, mask)
        acc_ref[...] += jnp.concatenate([_dot(a[hrows(h)].astype(BF16), v_of(h)) for h in range(H_B)], axis=0)
        c_ref[...] = jnp.broadcast_to(c, c_ref.shape)
        live_ref[0] = (jnp.max(c) > SB_LOG_FLOOR).astype(jnp.int32)

    return visit


def _visit_pages(visit, k_refs, v_refs, live_ref):
    for k_ref, v_ref in zip(k_refs, v_refs):
        @pl.when(live_ref[0] > 0)
        def _(k_ref=k_ref, v_ref=v_ref):
            visit(lambda h: _page_head(k_ref, h, H_B).astype(BF16),
                  lambda h: _page_head(v_ref, h, H_B).astype(BF16), None)


def _sb_sample_head_kernel(pt_ref, q_ref, kn_ref, vn_ref, *refs, t, pg):
    k_refs, v_refs = refs[:pg], refs[pg:2 * pg]
    acc_ref, c_ref, flag_ref, live_ref = refs[2 * pg:]
    del pt_ref
    visit = _sb_visitor(q_ref, acc_ref, c_ref, live_ref, t)
    acc_ref[...] = jnp.zeros_like(acc_ref)
    c_ref[...] = jnp.zeros_like(c_ref)
    kn, vn = kn_ref[...], vn_ref[...]
    r_t = lax.broadcasted_iota(jnp.int32, (H_B * t, PAGE_SIZE), 0) % t
    c_j = lax.broadcasted_iota(jnp.int32, (H_B * t, PAGE_SIZE), 1)
    visit(lambda h: _pad_rows(_head_cols(kn, h), PAGE_SIZE).astype(BF16),
          lambda h: _pad_rows(_head_cols(vn, h), PAGE_SIZE).astype(BF16), c_j < r_t)
    _visit_pages(visit, k_refs, v_refs, live_ref)
    flag_ref[...] = jnp.full(flag_ref.shape, live_ref[0], jnp.int32)


def _sb_sample_tail_kernel(pt_ref, flag_s_ref, q_ref, acc_in_ref, c_in_ref, *refs, t, pg):
    k_refs, v_refs = refs[:pg], refs[pg:2 * pg]
    o_ref, acc_ref, c_ref, live_ref = refs[2 * pg:]
    del pt_ref
    s = pl.program_id(1)

    @pl.when(s == 0)
    def _():
        acc_ref[...] = acc_in_ref[...]
        c_ref[...] = c_in_ref[...]
        live_ref[0] = flag_s_ref[pl.program_id(0)]

    @pl.when(live_ref[0] > 0)
    def _():
        visit = _sb_visitor(q_ref, acc_ref, c_ref, live_ref, t)
        _visit_pages(visit, k_refs, v_refs, live_ref)

    @pl.when(s == pl.num_programs(1) - 1)
    def _():
        o_ref[...] = _unstack_heads(acc_ref[...], H_B, t)


def sb_sample_attn(z, pool_k, pool_v, page_table, layer, n_dec, t, *, head_pages=2, tail_pg=14):
    n_pages = page_table.shape[1]
    head_pages = min(head_pages, n_pages)
    rows = H_B * t
    page_block = (1, 1, PAGE_SIZE * H_B, D_HEAD)
    acc_spec = lambda *_: pl.BlockSpec((rows, LANE), lambda b, *_: (b, 0))
    head_page = lambda i: pl.BlockSpec(page_block, lambda b, pt: (layer, pt[b, n_pages - 1 - i], 0, 0))
    acc, c, flag = pl.pallas_call(
        functools.partial(_sb_sample_head_kernel, t=t, pg=head_pages),
        grid_spec=pltpu.PrefetchScalarGridSpec(
            num_scalar_prefetch=1,
            grid=(n_dec,),
            in_specs=[pl.BlockSpec((t, W_B), lambda b, pt, cb=cb: (b, cb * LANE // W_B)) for cb in (CB_QB, CB_KB, CB_VB)]
            + [head_page(i) for i in range(head_pages)] * 2,
            out_specs=[acc_spec(), acc_spec(), pl.BlockSpec((8, LANE), lambda b, pt: (b, 0))],
            scratch_shapes=[pltpu.SMEM((1,), jnp.int32)],
        ),
        out_shape=[jax.ShapeDtypeStruct((n_dec * rows, LANE), F32), jax.ShapeDtypeStruct((n_dec * rows, LANE), F32),
                   jax.ShapeDtypeStruct((n_dec * 8, LANE), jnp.int32)],
        compiler_params=_params("parallel"),
        name="sb_sample_head",
    )(page_table, z, z, z, *([pool_k] * head_pages), *([pool_v] * head_pages))
    n_tail = n_pages - head_pages
    assert n_tail > 0
    tail_pg = math.gcd(n_tail, tail_pg)
    steps = n_tail // tail_pg
    live = flag[::8, 0]

    def tail_page(i):
        def index(b, s, pt, lv):
            pos = n_pages - 1 - (head_pages + s * tail_pg + i)
            return (layer, jnp.where(lv[b] > 0, pt[b, pos], 0), 0, 0)
        return pl.BlockSpec(page_block, index)

    return pl.pallas_call(
        functools.partial(_sb_sample_tail_kernel, t=t, pg=tail_pg),
        grid_spec=pltpu.PrefetchScalarGridSpec(
            num_scalar_prefetch=2,
            grid=(n_dec, steps),
            in_specs=[pl.BlockSpec((t, W_B), lambda b, s, pt, lv: (b, CB_QB * LANE // W_B)),
                      pl.BlockSpec((rows, LANE), lambda b, s, pt, lv: (b, 0)),
                      pl.BlockSpec((rows, LANE), lambda b, s, pt, lv: (b, 0))]
            + [tail_page(i) for i in range(tail_pg)] * 2,
            out_specs=pl.BlockSpec((t, W_B), lambda b, s, pt, lv: (b, 0)),
            scratch_shapes=[pltpu.VMEM((rows, LANE), F32), pltpu.VMEM((rows, LANE), F32), pltpu.SMEM((1,), jnp.int32)],
        ),
        out_shape=jax.ShapeDtypeStruct((n_dec * t, W_B), F32),
        compiler_params=_params("parallel", "arbitrary"),
        name="sb_sample_tail",
    )(page_table, live, z, acc, c, *([pool_k] * tail_pg), *([pool_v] * tail_pg))


def _gla_kernel(q_ref, k_ref, v_ref, rc_ref, w2_ref, b_ref, s0_ref, o_ref, sfin_ref, st_ref, *, c, nchunk, ns):
    step = pl.program_id(1)
    hblock = lambda h: (slice(h * DK_C, (h + 1) * DK_C), slice(h * DV_C, (h + 1) * DV_C))

    @pl.when(step == 0)
    def _():
        st_ref[...] = jnp.zeros_like(st_ref)
        for u in range(ns):
            for h in range(H_C):
                st_ref[(u,) + hblock(h)] = s0_ref[u, h]

    r_i = lax.broadcasted_iota(jnp.int32, (c, c), 0)
    c_i = lax.broadcasted_iota(jnp.int32, (c, c), 1)
    tril = jnp.where(c_i <= r_i, 1.0, 0.0).astype(F32)
    s_idx = lax.broadcasted_iota(jnp.int32, (c, WK_C), 0)
    kh = lax.broadcasted_iota(jnp.int32, (WK_C, WV_C), 0) // DK_C
    vh = lax.broadcasted_iota(jnp.int32, (WK_C, WV_C), 1) // DV_C
    diag = kh == vh
    expand = jnp.where(diag, 1.0, 0.0).astype(BF16)
    pad_row = lax.broadcasted_iota(jnp.int32, (LANE, WK_C), 0)
    pad_col = lax.broadcasted_iota(jnp.int32, (WK_C, LANE), 1)

    def chunk(ci, carry):
        for u in range(ns):
            chunk_of(u, ci)
        return carry

    def chunk_of(u, ci):
        r0 = pl.multiple_of(ci * c, c)
        q = q_ref[u, pl.ds(r0, c), :] * (DK_C ** -0.5)
        k = k_ref[u, pl.ds(r0, c), :]
        v = v_ref[u, pl.ds(r0, c), :]
        x = _dot(rc_ref[u, pl.ds(r0, c), :].astype(BF16), w2_ref[...]) + b_ref[...]
        la = (jnp.minimum(x, 0.0) - jnp.log(1.0 + jnp.exp(-jnp.abs(x)))) * (1.0 / GLA_GATE_NORM)
        bc = _dot(tril, la, precision=HIGHEST)
        ps = []
        for t in range(c):
            diff = bc[t:t + 1, :] - bc
            dec = jnp.exp(jnp.where(s_idx <= t, diff, -jnp.inf))
            ps.append((q[t:t + 1, :] * dec) * k)
        a = _dot(jnp.concatenate(ps, axis=0).astype(BF16), expand)
        o_rows = [jnp.sum(a[t * c:(t + 1) * c, :] * v, axis=0, keepdims=True) for t in range(c)]
        st = st_ref[u]
        o = jnp.concatenate(o_rows, axis=0) + _dot((q * jnp.exp(bc)).astype(BF16), st.astype(BF16))
        o_ref[u, pl.ds(r0, c), :] = o
        b_last = bc[c - 1:c, :]
        kd = k * jnp.exp(b_last - bc)
        packed = jnp.where(pad_row == c, jnp.broadcast_to(b_last, (LANE, WK_C)), _pad_rows(kd, LANE))
        packed_t = packed.T
        g_col = jnp.exp(packed_t[:, c:c + 1])
        kd_t = jnp.where(pad_col < c, packed_t, 0.0).astype(BF16)
        ds = _dot(kd_t, _pad_rows(v, LANE).astype(BF16))
        st_ref[u] = st * g_col + jnp.where(diag, ds, 0.0)

    lax.fori_loop(0, nchunk, chunk, 0, unroll=min(2, nchunk))

    @pl.when(step == pl.num_programs(1) - 1)
    def _():
        for u in range(ns):
            for h in range(H_C):
                sfin_ref[u, h] = st_ref[(u,) + hblock(h)]


def gla(z, w2, b, s0, n, length, *, tb=256, ns=2):
    c = math.gcd(length, GLA_CHUNK)
    tb = math.gcd(length, tb)
    ns = math.gcd(n, ns)
    steps = length // tb
    w2p = jnp.concatenate([w2, jnp.zeros((LANE - GLA_RANK, WK_C), w2.dtype)], axis=0).astype(BF16)

    z3 = z.reshape(n, length, z.shape[1])
    rows = lambda width, cb: pl.BlockSpec((ns, tb, width), lambda g, s: (g, s, cb * LANE // width))
    o, sfin = pl.pallas_call(
        functools.partial(_gla_kernel, c=c, nchunk=tb // c, ns=ns),
        grid=(n // ns, steps),
        in_specs=[
            rows(WK_C, CB_QC), rows(WK_C, CB_KC), rows(WV_C, CB_VC), rows(LANE, CB_RC),
            pl.BlockSpec((LANE, WK_C), lambda g, s: (0, 0)),
            pl.BlockSpec((1, WK_C), lambda g, s: (0, 0)),
            pl.BlockSpec((ns, H_C, DK_C, DV_C), lambda g, s: (g, 0, 0, 0)),
        ],
        out_specs=[
            pl.BlockSpec((ns, tb, WV_C), lambda g, s: (g, s, 0)),
            pl.BlockSpec((ns, H_C, DK_C, DV_C), lambda g, s: (g, 0, 0, 0)),
        ],
        out_shape=[jax.ShapeDtypeStruct((n, length, WV_C), F32), jax.ShapeDtypeStruct((n, H_C, DK_C, DV_C), F32)],
        scratch_shapes=[pltpu.VMEM((ns, WK_C, WV_C), F32)],
        compiler_params=_params("parallel", "arbitrary"),
        name="gla",
    )(z3, z3, z3, z3, w2p, b.reshape(1, WK_C), s0)
    return o.reshape(n * length, WV_C), sfin


def _branch_kernel(ma_ref, mb_ref, mc_ref, ga_ref, gb_ref, gc_ref, oa_ref, ob_ref, oc_ref, gain_ref,
                   wa_ref, wb_ref, wc_ref, m_ref):
    silu = lambda g: g * _sigmoid(g)
    ya = _dot((oa_ref[...] * silu(ga_ref[...])).astype(BF16), wa_ref[...])
    yb = _dot((ob_ref[...] * silu(gb_ref[...])).astype(BF16), wb_ref[...])
    oc = oc_ref[...]
    gain = gain_ref[...]
    heads = []
    for h in range(H_C):
        oh = oc[:, h * DV_C:(h + 1) * DV_C]
        heads.append(oh * lax.rsqrt(jnp.mean(oh * oh, axis=-1, keepdims=True) + RMS_EPS) * gain)
    ocn = jnp.concatenate(heads, axis=-1)
    yc = _dot((ocn * silu(gc_ref[...])).astype(BF16), wc_ref[...])
    m = _sigmoid(ma_ref[...]) * ya + _sigmoid(mb_ref[...]) * yb + _sigmoid(mc_ref[...]) * yc
    m_ref[...] = m.astype(BF16)


def branch_merge(z, oa, ob, oc, gla_gain, wa, wb, wc, layer, *, tm=256):
    m = z.shape[0]
    tm = math.gcd(m, tm)
    zb = lambda cb, w: pl.BlockSpec((tm, w), lambda i: (i, cb * LANE // w))
    rows = lambda w: pl.BlockSpec((tm, w), lambda i: (i, 0))
    wspec = lambda w: pl.BlockSpec((None, w, D_MODEL), lambda i: (layer, 0, 0))
    return pl.pallas_call(
        _branch_kernel,
        grid=(m // tm,),
        in_specs=[zb(CB_MA, D_MODEL), zb(CB_MB, D_MODEL), zb(CB_MC, D_MODEL),
                  zb(CB_GA, W_A), zb(CB_GB, W_B), zb(CB_GC, WV_C),
                  rows(W_A), rows(W_B), rows(WV_C), pl.BlockSpec((1, DV_C), lambda i: (0, 0)),
                  wspec(W_A), wspec(W_B), wspec(WV_C)],
        out_specs=rows(D_MODEL),
        out_shape=jax.ShapeDtypeStruct((m, D_MODEL), BF16),
        compiler_params=_params("parallel"),
        name="branch_merge",
    )(z, z, z, z, z, z, oa, ob, oc, gla_gain.reshape(1, DV_C), wa, wb, wc)


def _outproj_kernel(x_ref, m_ref, w_ref, g_ref, o_ref, *, final):
    y = x_ref[...] + _dot(m_ref[...], w_ref[...])
    if final:
        y = y * lax.rsqrt(jnp.mean(y * y, axis=-1, keepdims=True) + RMS_EPS) * g_ref[...]
    o_ref[...] = y


def out_project(x, m, w_out, final_gain, layer, *, final, tm=512):
    rows, d = x.shape
    tm = math.gcd(rows, tm)
    return pl.pallas_call(
        functools.partial(_outproj_kernel, final=final),
        grid=(rows // tm,),
        in_specs=[pl.BlockSpec((tm, d), lambda i: (i, 0)), pl.BlockSpec((tm, d), lambda i: (i, 0)),
                  pl.BlockSpec((None, d, d), lambda i: (layer, 0, 0)), pl.BlockSpec((1, d), lambda i: (0, 0))],
        out_specs=pl.BlockSpec((tm, d), lambda i: (i, 0)),
        out_shape=jax.ShapeDtypeStruct((rows, d), F32),
        compiler_params=_params("parallel"),
        name="out_project",
    )(x, m, w_out, final_gain.reshape(1, d))


def _kv_writer_kernel(*refs, heads, n_layers):
    nseg = len(heads)
    ins, outs = refs[:n_layers * nseg], refs[n_layers * nseg:]
    layer = pl.program_id(0)
    for li in range(n_layers):
        @pl.when(layer == li)
        def _(li=li):
            for sg, nh in enumerate(heads):
                src, dst = ins[li * nseg + sg], outs[sg]
                for h in range(nh):
                    dst[0, pl.ds(h, src.shape[0], stride=nh), :] = src[:, h * D_HEAD:(h + 1) * D_HEAD]


def kv_rows(zs, *, tm=512):
    n_layers = len(zs)
    rows = zs[0].shape[0]
    tm = math.gcd(rows, tm)
    nt = rows // tm
    segs = ((CB_KA, H_A), (CB_VA, H_A), (CB_KB, H_B), (CB_VB, H_B))
    heads = tuple(nh for _, nh in segs)

    def in_spec(li, cb, nh):
        w = nh * D_HEAD
        return pl.BlockSpec((tm, w), lambda l, i: (jnp.clip(i + (l - li) * nt, 0, nt - 1), cb * LANE // w))

    return pl.pallas_call(
        functools.partial(_kv_writer_kernel, heads=heads, n_layers=n_layers),
        grid=(n_layers, nt),
        in_specs=[in_spec(li, cb, nh) for li in range(n_layers) for cb, nh in segs],
        out_specs=[pl.BlockSpec((1, tm * nh, D_HEAD), lambda l, i: (l, i, 0)) for nh in heads],
        out_shape=[jax.ShapeDtypeStruct((n_layers, rows * nh, D_HEAD), F32) for nh in heads],
        compiler_params=_params("arbitrary", "arbitrary"),
        name="kv_rows",
    )(*[z for z in zs for _ in segs])


def kernel(x_prompt, x_sample, cache_moba_k, cache_moba_v, cache_sb_k, cache_sb_v, state_gla, page_table,
           norm_gain, w_in, gla_w2, gla_b, gla_out_gain, w_br_a, w_br_b, w_br_c, w_out, final_gain):
    n_seq, seq_len, _ = x_prompt.shape
    n_dec, dec_len, _ = x_sample.shape
    n_pool = cache_moba_k.shape[1]
    x_p = x_prompt.reshape(n_seq * seq_len, D_MODEL)
    x_s = x_sample.reshape(n_dec * dec_len, D_MODEL)
    w_all = jnp.concatenate(
        [w_in[:, :, MAIN_W + GLA_RANK:], w_in[:, :, :MAIN_W], w_in[:, :, MAIN_W:MAIN_W + GLA_RANK],
         jnp.zeros((DEPTH, D_MODEL, RC_PAD - GLA_RANK), w_in.dtype)], axis=-1).astype(BF16)
    wa, wb, wc, wo = (w.astype(BF16) for w in (w_br_a, w_br_b, w_br_c, w_out))
    pool_ak = cache_moba_k.reshape(DEPTH, n_pool, PAGE_SIZE * H_A, D_HEAD)
    pool_av = cache_moba_v.reshape(DEPTH, n_pool, PAGE_SIZE * H_A, D_HEAD)
    pool_bk = cache_sb_k.reshape(DEPTH, n_pool, PAGE_SIZE * H_B, D_HEAD)
    pool_bv = cache_sb_v.reshape(DEPTH, n_pool, PAGE_SIZE * H_B, D_HEAD)
    zero_state = jnp.zeros((n_seq, H_C, DK_C, DV_C), F32)

    zs_p, zs_s, st_p, st_s = [], [], [], []
    for l in range(DEPTH):
        last = l == DEPTH - 1
        z = norm_project(x_p, norm_gain[l], w_all, l)
        oa = moba_prompt_attn(z, n_seq, seq_len)
        ob = sb_prompt_attn(z, n_seq, seq_len)
        oc, sf = gla(z, gla_w2[l], gla_b[l], zero_state, n_seq, seq_len)
        m = branch_merge(z, oa, ob, oc, gla_out_gain[l], wa, wb, wc, l)
        x_p = out_project(x_p, m, wo, final_gain, l, final=last)
        zs_p.append(z)
        st_p.append(sf)
        z = norm_project(x_s, norm_gain[l], w_all, l)
        oa = moba_sample_attn(z, pool_ak, pool_av, page_table, l, n_dec, dec_len)
        ob = sb_sample_attn(z, pool_bk, pool_bv, page_table, l, n_dec, dec_len)
        oc, sf = gla(z, gla_w2[l], gla_b[l], state_gla[l].astype(F32), n_dec, dec_len)
        m = branch_merge(z, oa, ob, oc, gla_out_gain[l], wa, wb, wc, l)
        x_s = out_project(x_s, m, wo, final_gain, l, final=last)
        zs_s.append(z)
        st_s.append(sf)

    kv_p = kv_rows(zs_p)
    kv_s = kv_rows(zs_s)
    shape_p = lambda a, nh: a.reshape(DEPTH, n_seq, seq_len, nh, D_HEAD)
    shape_s = lambda a, nh: a.reshape(DEPTH, n_dec, dec_len, nh, D_HEAD)
    return (x_p.reshape(n_seq, seq_len, D_MODEL), x_s.reshape(n_dec, dec_len, D_MODEL),
            shape_p(kv_p[0], H_A), shape_p(kv_p[1], H_A), shape_p(kv_p[2], H_B), shape_p(kv_p[3], H_B),
            jnp.stack(st_p),
            shape_s(kv_s[0], H_A), shape_s(kv_s[1], H_A), shape_s(kv_s[2], H_B), shape_s(kv_s[3], H_B),
            jnp.stack(st_s))
```
